```python
import math
import jax, jax.numpy as jnp
from jax import lax
import numpy as np

D_MODEL = 1024
BATCH = 32
SEQ = 256
DEPTH = 2
DEC_BATCH = 2
DEC_SEQ = 4096
PAST_LEN = 512

GRID_W = 64
POS_BASE = 10000.0
EPS = 1e-6
D_FF = 4 * D_MODEL
MIX_WIDTH = D_MODEL
N_MOD = 6

S5_CH = 3 * MIX_WIDTH // 4
S5_GROUP = 16
S5_GROUPS = S5_CH // S5_GROUP
S5_STATE = 64
DT_MIN = 1e-3
DT_MAX = 1e-1
FNET_CH = MIX_WIDTH - S5_CH
FNET_GROUPS = 4
FNET_GW = FNET_CH // FNET_GROUPS

POOL_WINDOWS = (2, 4, 8, 16)
POOL_GROUPS = len(POOL_WINDOWS)
POOL_CH = MIX_WIDTH // 2
POOL_GW = POOL_CH // POOL_GROUPS
GMLP_CH = MIX_WIDTH - POOL_CH
GMLP_HEADS = 4
GMLP_HD = GMLP_CH // GMLP_HEADS
GMLP_CHUNK = 128

kernel_name = "hybrid_s5_fnet_pool_gmlp_diffusion_step"

F32 = jnp.float32


def rmsnorm(x, g):
    xf = x.astype(F32)
    y = xf * lax.rsqrt(jnp.mean(xf * xf, axis=-1, keepdims=True) + EPS)
    return (y * g.astype(F32)).astype(x.dtype)


def group_layernorm(v, g, b, groups):
    bt, l, cn = v.shape
    vf = v.astype(F32).reshape(bt, l, groups, cn // groups)
    mu = jnp.mean(vf, axis=-1, keepdims=True)
    var = jnp.mean(jnp.square(vf - mu), axis=-1, keepdims=True)
    y = ((vf - mu) * lax.rsqrt(var + EPS)).reshape(bt, l, cn)
    return (y * g.astype(F32) + b.astype(F32)).astype(v.dtype)


def grid_pos_embed(length, dtype):
    rows = length // GRID_W
    rr, cc = jnp.meshgrid(jnp.arange(rows, dtype=F32), jnp.arange(GRID_W, dtype=F32), indexing="ij")
    quarter = D_MODEL // 4
    omega = 1.0 / (POS_BASE ** (jnp.arange(quarter, dtype=F32) / quarter))

    def axis_embed(p):
        ang = p.reshape(-1)[:, None] * omega[None, :]
        return jnp.concatenate([jnp.sin(ang), jnp.cos(ang)], axis=-1)

    return jnp.concatenate([axis_embed(rr), axis_embed(cc)], axis=-1).astype(dtype)


def modulation(cond, w_mod, b_mod):
    m = jax.nn.silu(cond) @ w_mod + b_mod
    return jnp.split(m[:, None, :], N_MOD, axis=-1)


def sq_relu_mlp(h, w1, w2):
    return jnp.square(jax.nn.relu(h @ w1)) @ w2


def s5_discretize(lam_re, lam_im, log_dt, b_re, b_im):
    dt = jnp.exp(log_dt)[:, None]
    mag = jnp.exp(lam_re * dt)
    ab_re = mag * jnp.cos(lam_im * dt)
    ab_im = mag * jnp.sin(lam_im * dt)
    num_re = ab_re - 1.0
    num_im = ab_im
    den = lam_re * lam_re + lam_im * lam_im
    coef_re = (num_re * lam_re + num_im * lam_im) / den
    coef_im = (num_im * lam_re - num_re * lam_im) / den
    bb_re = coef_re[..., None] * b_re - coef_im[..., None] * b_im
    bb_im = coef_re[..., None] * b_im + coef_im[..., None] * b_re
    return ab_re, ab_im, bb_re, bb_im


def _complex_linear_combine(e1, e2):
    a1r, a1i, b1r, b1i = e1
    a2r, a2i, b2r, b2i = e2
    return (a2r * a1r - a2i * a1i,
            a2r * a1i + a2i * a1r,
            a2r * b1r - a2i * b1i + b2r,
            a2r * b1i + a2i * b1r + b2i)


def s5_direction(u, h0_re, h0_im, lam_re, lam_im, log_dt, b_re, b_im, c_re, c_im):
    ab_re, ab_im, bb_re, bb_im = s5_discretize(lam_re, lam_im, log_dt, b_re, b_im)
    bu_re = jnp.einsum("blgh,gph->blgp", u, bb_re)
    bu_im = jnp.einsum("blgh,gph->blgp", u, bb_im)
    bu_re = bu_re.at[:, 0].add(ab_re * h0_re - ab_im * h0_im)
    bu_im = bu_im.at[:, 0].add(ab_re * h0_im + ab_im * h0_re)
    a_re = jnp.broadcast_to(ab_re, bu_re.shape)
    a_im = jnp.broadcast_to(ab_im, bu_im.shape)
    _, _, h_re, h_im = lax.associative_scan(_complex_linear_combine, (a_re, a_im, bu_re, bu_im), axis=1)
    y = jnp.einsum("blgp,ghp->blgh", h_re, c_re) - jnp.einsum("blgp,ghp->blgh", h_im, c_im)
    return y, h_re[:, -1], h_im[:, -1]


def s5_mixer(u, h0_re, h0_im, lam_re, lam_im, log_dt, b_re, b_im, c_re, c_im, d, w_glu, b_glu):
    bt, l, _ = u.shape
    lam_re, lam_im, log_dt, b_re, b_im, c_re, c_im, d = (
        a.astype(F32) for a in (lam_re, lam_im, log_dt, b_re, b_im, c_re, c_im, d))
    h0_re = h0_re.astype(F32)
    h0_im = h0_im.astype(F32)
    uf = u.astype(F32).reshape(bt, l, S5_GROUPS, S5_GROUP)
    y_f, hf_re, hf_im = s5_direction(uf, h0_re[:, 0], h0_im[:, 0], lam_re[0], lam_im[0], log_dt[0],
                                     b_re[0], b_im[0], c_re[0], c_im[0])
    y_b, hb_re, hb_im = s5_direction(jnp.flip(uf, 1), h0_re[:, 1], h0_im[:, 1], lam_re[1], lam_im[1], log_dt[1],
                                     b_re[1], b_im[1], c_re[1], c_im[1])
    y = (y_f + jnp.flip(y_b, 1)).reshape(bt, l, S5_CH) + d * uf.reshape(bt, l, S5_CH)
    g = jax.nn.gelu(y)
    out = g * jax.nn.sigmoid(g @ w_glu.astype(F32) + b_glu.astype(F32))
    s_re = jnp.stack([hf_re, hb_re], axis=1)
    s_im = jnp.stack([hf_im, hb_im], axis=1)
    return out.astype(u.dtype), s_re, s_im


def fnet_mixer(u, w, b):
    bt, l, _ = u.shape
    uf = u.astype(F32).reshape(bt, l, FNET_GROUPS, FNET_GW)
    z = jnp.fft.fft2(uf, axes=(1, 3), norm="ortho").real.astype(u.dtype)
    out = jnp.einsum("blgc,gcd->blgd", z, w) + b
    return out.reshape(bt, l, FNET_CH)


def pool_mixer(u, w, scale):
    bt, l, _ = u.shape
    uf = u.astype(F32).reshape(bt, l, POOL_GROUPS, POOL_GW)
    cs = jnp.concatenate([jnp.zeros((bt, 1, POOL_GROUPS, POOL_GW), F32), jnp.cumsum(uf, axis=1)], axis=1)
    t = jnp.arange(l)
    pooled = []
    for gi, win in enumerate(POOL_WINDOWS):
        lo = jnp.clip(t - win // 2, 0, l)
        hi = jnp.clip(t + win // 2, 0, l)
        s = cs[:, hi, gi] - cs[:, lo, gi]
        pooled.append(s / (hi - lo).astype(F32)[None, :, None])
    p = (jnp.stack(pooled, axis=2) - uf).astype(u.dtype)
    mixed = jnp.einsum("blgc,gcd->blgd", p, w).reshape(bt, l, POOL_CH)
    return mixed * scale


def gmlp_mixer(uv, ln_g, ln_b, ws, bs):
    bt, l, _ = uv.shape
    z = jax.nn.gelu(uv)
    u, v = jnp.split(z, 2, axis=-1)
    v = group_layernorm(v, ln_g, ln_b, GMLP_HEADS)
    n_chunks = l // GMLP_CHUNK
    vh = v.reshape(bt, n_chunks, GMLP_CHUNK, GMLP_HEADS, GMLP_HD)
    sv = jnp.einsum("bnkhd,hqk->bnqhd", vh, ws) + bs.T[:, :, None]
    return u * sv.reshape(bt, l, GMLP_CH)


def even_mixer(h, h0_re, h0_im, w_in, w_out, lam_re, lam_im, log_dt, b_re, b_im, c_re, c_im, d,
               w_glu, b_glu, fnet_w, fnet_b):
    z = h @ w_in
    ya, s_re, s_im = s5_mixer(z[..., :S5_CH], h0_re, h0_im, lam_re, lam_im, log_dt, b_re, b_im,
                              c_re, c_im, d, w_glu, b_glu)
    yb = fnet_mixer(z[..., S5_CH:], fnet_w, fnet_b)
    return jnp.concatenate([ya, yb], axis=-1) @ w_out, (s_re, s_im)


def odd_mixer(h, w_in, w_out, pool_w, pool_scale, ln_g, ln_b, ws, bs):
    z = h @ w_in
    yc = pool_mixer(z[..., :POOL_CH], pool_w, pool_scale)
    yd = gmlp_mixer(z[..., POOL_CH:], ln_g, ln_b, ws, bs)
    return jnp.concatenate([yc, yd], axis=-1) @ w_out, ()


def run_layer(x, cond, mixer, w_mod, b_mod, g_mix_pre, g_mix_post, g_ff_pre, g_ff_post, w_ff1, w_ff2):
    sh1, sc1, gt1, sh2, sc2, gt2 = modulation(cond, w_mod, b_mod)
    h = rmsnorm(x, g_mix_pre) * (1.0 + sc1) + sh1
    y, aux = mixer(h)
    x = x + gt1 * rmsnorm(y, g_mix_post)
    h = rmsnorm(x, g_ff_pre) * (1.0 + sc2) + sh2
    x = x + gt2 * rmsnorm(sq_relu_mlp(h, w_ff1, w_ff2), g_ff_post)
    return x, aux


def setup_inputs(seed: int = 0) -> dict:
    key = jax.random.key(seed)
    ks = iter(jax.random.split(key, 80))

    def nrm(shape, scale=1.0):
        return jax.random.normal(next(ks), shape, F32) * scale

    def gain(shape):
        return 1.0 + nrm(shape, 0.05)

    inp = {}
    inp["x_prompt"] = nrm((BATCH, SEQ, D_MODEL))
    inp["x_sample"] = nrm((DEC_BATCH, DEC_SEQ, D_MODEL))
    inp["state_l0_s5_re"] = nrm((DEC_BATCH, 2, S5_GROUPS, S5_STATE), 0.1)
    inp["state_l0_s5_im"] = nrm((DEC_BATCH, 2, S5_GROUPS, S5_STATE), 0.1)
    inp["c"] = nrm((DEC_BATCH, D_MODEL))
    inp["c_ctx"] = nrm((D_MODEL,))
    for li in range(DEPTH):
        p = "l%d_" % li
        inp[p + "w_mod"] = nrm((D_MODEL, N_MOD * D_MODEL), 0.5 * D_MODEL ** -0.5)
        inp[p + "b_mod"] = nrm((N_MOD * D_MODEL,), 0.02)
        inp[p + "g_mix_pre"] = gain((D_MODEL,))
        inp[p + "g_mix_post"] = gain((D_MODEL,))
        inp[p + "g_ff_pre"] = gain((D_MODEL,))
        inp[p + "g_ff_post"] = gain((D_MODEL,))
        inp[p + "w_ff1"] = nrm((D_MODEL, D_FF), D_MODEL ** -0.5)
        inp[p + "w_ff2"] = nrm((D_FF, D_MODEL), D_FF ** -0.5)
        if li % 2 == 0:
            inp[p + "w_in"] = nrm((D_MODEL, MIX_WIDTH), D_MODEL ** -0.5)
            inp[p + "w_out"] = nrm((MIX_WIDTH, D_MODEL), MIX_WIDTH ** -0.5)
            n_idx = jnp.arange(S5_STATE, dtype=F32)
            inp[p + "s5_lambda_re"] = -0.5 + nrm((2, S5_GROUPS, S5_STATE), 0.01)
            inp[p + "s5_lambda_im"] = math.pi * n_idx + nrm((2, S5_GROUPS, S5_STATE), 0.01)
            inp[p + "s5_log_dt"] = jax.random.uniform(next(ks), (2, S5_GROUPS), F32,
                                                      math.log(DT_MIN), math.log(DT_MAX))
            inp[p + "s5_b_re"] = nrm((2, S5_GROUPS, S5_STATE, S5_GROUP), (2 * S5_GROUP) ** -0.5)
            inp[p + "s5_b_im"] = nrm((2, S5_GROUPS, S5_STATE, S5_GROUP), (2 * S5_GROUP) ** -0.5)
            inp[p + "s5_c_re"] = nrm((2, S5_GROUPS, S5_GROUP, S5_STATE), (2 * S5_STATE) ** -0.5)
            inp[p + "s5_c_im"] = nrm((2, S5_GROUPS, S5_GROUP, S5_STATE), (2 * S5_STATE) ** -0.5)
            inp[p + "s5_d"] = nrm((S5_CH,))
            inp[p + "s5_w_glu"] = nrm((S5_CH, S5_CH), S5_CH ** -0.5)
            inp[p + "s5_b_glu"] = nrm((S5_CH,), 0.02)
            inp[p + "fnet_w"] = nrm((FNET_GROUPS, FNET_GW, FNET_GW), FNET_GW ** -0.5)
            inp[p + "fnet_b"] = nrm((FNET_GROUPS, FNET_GW), 0.02)
        else:
            inp[p + "w_in"] = nrm((D_MODEL, POOL_CH + 2 * GMLP_CH), D_MODEL ** -0.5)
            inp[p + "w_out"] = nrm((MIX_WIDTH, D_MODEL), MIX_WIDTH ** -0.5)
            inp[p + "pool_w"] = nrm((POOL_GROUPS, POOL_GW, POOL_GW), POOL_GW ** -0.5)
            inp[p + "pool_scale"] = 1.0 + nrm((POOL_CH,), 0.1)
            inp[p + "gmlp_ln_g"] = gain((GMLP_CH,))
            inp[p + "gmlp_ln_b"] = nrm((GMLP_CH,), 0.02)
            inp[p + "gmlp_ws"] = nrm((GMLP_HEADS, GMLP_CHUNK, GMLP_CHUNK), GMLP_CHUNK ** -0.5)
            inp[p + "gmlp_bs"] = 1.0 + nrm((GMLP_HEADS, GMLP_CHUNK), 0.1)
    return inp


def reference(x_prompt, x_sample, state_l0_s5_re, state_l0_s5_im, c, c_ctx,
              l0_w_mod, l0_b_mod, l0_g_mix_pre, l0_g_mix_post, l0_g_ff_pre, l0_g_ff_post, l0_w_ff1, l0_w_ff2,
              l0_w_in, l0_w_out, l0_s5_lambda_re, l0_s5_lambda_im, l0_s5_log_dt, l0_s5_b_re, l0_s5_b_im,
              l0_s5_c_re, l0_s5_c_im, l0_s5_d, l0_s5_w_glu, l0_s5_b_glu, l0_fnet_w, l0_fnet_b,
              l1_w_mod, l1_b_mod, l1_g_mix_pre, l1_g_mix_post, l1_g_ff_pre, l1_g_ff_post, l1_w_ff1, l1_w_ff2,
              l1_w_in, l1_w_out, l1_pool_w, l1_pool_scale, l1_gmlp_ln_g, l1_gmlp_ln_b, l1_gmlp_ws, l1_gmlp_bs):
    common = (
        (l0_w_mod, l0_b_mod, l0_g_mix_pre, l0_g_mix_post, l0_g_ff_pre, l0_g_ff_post, l0_w_ff1, l0_w_ff2),
        (l1_w_mod, l1_b_mod, l1_g_mix_pre, l1_g_mix_post, l1_g_ff_pre, l1_g_ff_post, l1_w_ff1, l1_w_ff2),
    )
    even_params = (l0_w_in, l0_w_out, l0_s5_lambda_re, l0_s5_lambda_im, l0_s5_log_dt, l0_s5_b_re, l0_s5_b_im,
                   l0_s5_c_re, l0_s5_c_im, l0_s5_d, l0_s5_w_glu, l0_s5_b_glu, l0_fnet_w, l0_fnet_b)
    odd_params = (l1_w_in, l1_w_out, l1_pool_w, l1_pool_scale, l1_gmlp_ln_g, l1_gmlp_ln_b, l1_gmlp_ws, l1_gmlp_bs)

    def trunk(x, cond, h0_re, h0_im):
        states = []
        for i in range(DEPTH):
            if i % 2 == 0:
                mixer = lambda h: even_mixer(h, h0_re, h0_im, *even_params)
            else:
                mixer = lambda h: odd_mixer(h, *odd_params)
            x, aux = run_layer(x, cond, mixer, *common[i])
            states.extend(aux)
        return x, states

    zero_state = jnp.zeros((x_prompt.shape[0], 2, S5_GROUPS, S5_STATE), F32)
    y_prompt, ctx_states = trunk(x_prompt, c_ctx[None, :], zero_state, zero_state)
    new_l0_s5_re = ctx_states[0].astype(x_prompt.dtype)
    new_l0_s5_im = ctx_states[1].astype(x_prompt.dtype)

    xs = x_sample + grid_pos_embed(x_sample.shape[1], x_sample.dtype)[None]
    y_sample, _ = trunk(xs, c, state_l0_s5_re, state_l0_s5_im)

    return (y_prompt, y_sample, new_l0_s5_re, new_l0_s5_im)
```

```python
import functools
import math

import jax
import jax.numpy as jnp
import numpy as np
from jax import lax
from jax.experimental import pallas as pl
from jax.experimental.pallas import tpu as pltpu

F32 = jnp.float32
BF16 = jnp.bfloat16
I32 = jnp.int32

D = 1024
D_FF = 4 * D
N_MOD = 6
EPS = 1e-6
GRID_W = 64
POS_BASE = 10000.0

S5_CH = 768
S5_GROUPS = 48
S5_GROUP = 16
S5_STATE = 64
S5_CLUSTERS = 3
CL_CH = 256
CL_STATES = 1024
FNET_CH = 256
FNET_GW = 64

POOL_WINDOWS = (2, 4, 8, 16)
POOL_CH = 512
POOL_GW = 128
POOL_HALO = 8
GMLP_CH = 512
GMLP_HEADS = 4
GMLP_HD = 128
GMLP_CHUNK = 128

LANES = 128
SUBLANES = 8
VMEM_LIMIT = 56 * 1024 * 1024

S5_TT = 128
S5_PITCH = S5_TT + SUBLANES


def _cparams(sem):
    return pltpu.CompilerParams(dimension_semantics=sem, vmem_limit_bytes=VMEM_LIMIT)


def _rms(x, g):
    return x * lax.rsqrt(jnp.mean(x * x, axis=-1, keepdims=True) + EPS) * g


def _bdot(a, b):
    return jnp.dot(a.astype(BF16), b.astype(BF16), preferred_element_type=F32)


def _mod_kernel(c_ref, w_ref, b_ref, o_ref):
    c = c_ref[...]
    s = c * jax.nn.sigmoid(c)
    o_ref[...] = _bdot(s, w_ref[...]) + b_ref[...]


def _modulation(conds, w_mod, b_mod):
    tn = 1024
    out = pl.pallas_call(
        _mod_kernel,
        out_shape=jax.ShapeDtypeStruct((8, N_MOD * D), F32),
        grid=(N_MOD * D // tn,),
        in_specs=[pl.BlockSpec((8, D), lambda j: (0, 0)),
                  pl.BlockSpec((D, tn), lambda j: (0, j)),
                  pl.BlockSpec((1, tn), lambda j: (0, j))],
        out_specs=pl.BlockSpec((8, tn), lambda j: (0, j)),
        compiler_params=_cparams(("parallel",)),
        name="modulation",
    )(conds, w_mod, b_mod.reshape(1, -1))
    return out.reshape(8, 1, N_MOD * D)


def _embed_kernel(x_ref, o_ref, tab_ref, *, tm):
    i = pl.program_id(0)
    quarter = D // 4
    k = lax.broadcasted_iota(I32, (GRID_W, quarter), 1).astype(F32)
    p = lax.broadcasted_iota(I32, (GRID_W, quarter), 0).astype(F32)
    omega = jnp.exp(k * (-math.log(POS_BASE) / quarter))
    ang = p * omega
    tab_ref[:, 0:quarter] = jnp.sin(ang)
    tab_ref[:, quarter:2 * quarter] = jnp.cos(ang)
    rows_per_seq = 4096 // GRID_W
    row0 = (i * (tm // GRID_W)) % rows_per_seq
    for r in range(tm // GRID_W):
        sl = slice(r * GRID_W, (r + 1) * GRID_W)
        rowvec = tab_ref[pl.ds(row0 + r, 1), :]
        o_ref[sl, 0:2 * quarter] = x_ref[sl, 0:2 * quarter] + rowvec
        o_ref[sl, 2 * quarter:] = x_ref[sl, 2 * quarter:] + tab_ref[...]


def _embed(x, tm=512):
    n = x.shape[0]
    return pl.pallas_call(
        functools.partial(_embed_kernel, tm=tm),
        out_shape=jax.ShapeDtypeStruct((n, D), F32),
        grid=(n // tm,),
        in_specs=[pl.BlockSpec((tm, D), lambda i: (i, 0))],
        out_specs=pl.BlockSpec((tm, D), lambda i: (i, 0)),
        scratch_shapes=[pltpu.VMEM((GRID_W, D // 2), F32)],
        compiler_params=_cparams(("parallel",)),
        name="pos_embed",
    )(x)


def _inproj_kernel(x_ref, g_ref, m_ref, w_ref, o_ref, *, off):
    m = m_ref[0]
    sh = m[:, off * D:(off + 1) * D]
    sc = m[:, (off + 1) * D:(off + 2) * D]
    h = _rms(x_ref[...], g_ref[...]) * (1.0 + sc) + sh
    o_ref[...] = jnp.dot(h.astype(BF16), w_ref[...], preferred_element_type=F32)


def _inproj(x, g, mods, cond_map, w_bf16, tm=512):
    n = x.shape[0]
    nout = w_bf16.shape[1]
    return pl.pallas_call(
        functools.partial(_inproj_kernel, off=0),
        out_shape=jax.ShapeDtypeStruct((n, nout), F32),
        grid=(n // tm,),
        in_specs=[pl.BlockSpec((tm, D), lambda i: (i, 0)),
                  pl.BlockSpec((1, D), lambda i: (0, 0)),
                  pl.BlockSpec((1, 1, N_MOD * D), lambda i: (cond_map(i * tm), 0, 0)),
                  pl.BlockSpec((D, nout), lambda i: (0, 0))],
        out_specs=pl.BlockSpec((tm, nout), lambda i: (i, 0)),
        compiler_params=_cparams(("parallel",)),
        name="inproj",
    )(x, g.reshape(1, D), mods, w_bf16)


def _outproj_kernel(*refs, n_in, gate_off):
    ins = refs[:n_in]
    ws = refs[n_in:2 * n_in]
    x_ref, g_ref, m_ref, o_ref = refs[2 * n_in:]
    y = None
    for a_ref, w_ref in zip(ins, ws):
        t = jnp.dot(a_ref[...].astype(BF16), w_ref[...], preferred_element_type=F32)
        y = t if y is None else y + t
    gt = m_ref[0][:, gate_off * D:(gate_off + 1) * D]
    o_ref[...] = x_ref[...] + gt * _rms(y, g_ref[...])


def _outproj(parts, weights, x, g, mods, cond_map, tm=512):
    n = x.shape[0]
    n_in = len(parts)
    in_specs = [pl.BlockSpec((tm, p.shape[1]), lambda i: (i, 0)) for p in parts]
    in_specs += [pl.BlockSpec(w.shape, lambda i: (0, 0)) for w in weights]
    in_specs += [pl.BlockSpec((tm, D), lambda i: (i, 0)),
                 pl.BlockSpec((1, D), lambda i: (0, 0)),
                 pl.BlockSpec((1, 1, N_MOD * D), lambda i: (cond_map(i * tm), 0, 0))]
    return pl.pallas_call(
        functools.partial(_outproj_kernel, n_in=n_in, gate_off=2),
        out_shape=jax.ShapeDtypeStruct((n, D), F32),
        grid=(n // tm,),
        in_specs=in_specs,
        out_specs=pl.BlockSpec((tm, D), lambda i: (i, 0)),
        compiler_params=_cparams(("parallel",)),
        name="outproj",
    )(*parts, *weights, x, g.reshape(1, D), mods)


def _ffn_kernel(x_ref, gpre_ref, gpost_ref, m_ref, w1_ref, w2_ref, o_ref, h_ref, acc_ref):
    f = pl.program_id(1)
    m = m_ref[0]

    @pl.when(f == 0)
    def _():
        sh = m[:, 3 * D:4 * D]
        sc = m[:, 4 * D:5 * D]
        h_ref[...] = (_rms(x_ref[...], gpre_ref[...]) * (1.0 + sc) + sh).astype(BF16)
        acc_ref[...] = jnp.zeros_like(acc_ref)

    a = jnp.dot(h_ref[...], w1_ref[...], preferred_element_type=F32)
    a = jnp.square(jnp.maximum(a, 0.0))
    acc_ref[...] += jnp.dot(a.astype(BF16), w2_ref[...], preferred_element_type=F32)

    @pl.when(f == pl.num_programs(1) - 1)
    def _():
        gt = m[:, 5 * D:6 * D]
        o_ref[...] = x_ref[...] + gt * _rms(acc_ref[...], gpost_ref[...])


def _ffn(x, gpre, gpost, mods, cond_map, w1, w2, tm=512, tf=512):
    n = x.shape[0]
    return pl.pallas_call(
        _ffn_kernel,
        out_shape=jax.ShapeDtypeStruct((n, D), F32),
        grid=(n // tm, D_FF // tf),
        in_specs=[pl.BlockSpec((tm, D), lambda i, f: (i, 0)),
                  pl.BlockSpec((1, D), lambda i, f: (0, 0)),
                  pl.BlockSpec((1, D), lambda i, f: (0, 0)),
                  pl.BlockSpec((1, 1, N_MOD * D), lambda i, f: (cond_map(i * tm), 0, 0)),
                  pl.BlockSpec((D, tf), lambda i, f: (0, f)),
                  pl.BlockSpec((tf, D), lambda i, f: (f, 0))],
        out_specs=pl.BlockSpec((tm, D), lambda i, f: (i, 0)),
        scratch_shapes=[pltpu.VMEM((tm, D), BF16), pltpu.VMEM((tm, D), F32)],
        compiler_params=_cparams(("parallel", "arbitrary")),
        name="ffn",
    )(x, gpre.reshape(1, D), gpost.reshape(1, D), mods, w1, w2)


def _s5_disc_kernel(lre_ref, lim_ref, ldt_ref, bre_ref, bim_ref, are_ref, aim_ref, bbre_ref, bbim_ref):
    lam_re = lre_ref[...]
    lam_im = lim_ref[...]
    dt = jnp.exp(ldt_ref[...])
    mag = jnp.exp(lam_re * dt)
    ab_re = mag * jnp.cos(lam_im * dt)
    ab_im = mag * jnp.sin(lam_im * dt)
    num_re = ab_re - 1.0
    num_im = ab_im
    den = lam_re * lam_re + lam_im * lam_im
    coef_re = (num_re * lam_re + num_im * lam_im) / den
    coef_im = (num_im * lam_re - num_re * lam_im) / den
    b_re = bre_ref[...]
    b_im = bim_ref[...]
    are_ref[...] = ab_re
    aim_ref[...] = ab_im
    bbre_ref[...] = coef_re * b_re - coef_im * b_im
    bbim_ref[...] = coef_re * b_im + coef_im * b_re


def _s5_discretize(lam_re, lam_im, log_dt, b_re, b_im):
    rows = 2 * S5_GROUPS * S5_GROUP
    shp = (2, S5_GROUPS, S5_GROUP, S5_STATE)
    rep = lambda a: jnp.broadcast_to(a[:, :, None, :], shp).reshape(rows, S5_STATE)
    ldt = jnp.broadcast_to(log_dt[:, :, None, None], shp).reshape(rows, S5_STATE)
    tr = lambda b: jnp.swapaxes(b, 2, 3).reshape(rows, S5_STATE)
    spec = pl.BlockSpec((rows, S5_STATE), lambda: (0, 0))
    outs = pl.pallas_call(
        _s5_disc_kernel,
        out_shape=[jax.ShapeDtypeStruct((rows, S5_STATE), F32)] * 4,
        in_specs=[spec] * 5,
        out_specs=[spec] * 4,
        name="s5_discretize",
    )(rep(lam_re), rep(lam_im), ldt, tr(b_re), tr(b_im))
    return [o.reshape(shp) for o in outs]


def _s5_block_params(ab_re, ab_im, bb_re, bb_im, c_re, c_im):
    eye = jnp.eye(S5_GROUP, dtype=F32)
    g5 = (2, S5_CLUSTERS, 16, S5_GROUP, S5_STATE)

    def in_blk(bb):
        t = jnp.einsum("dcghp,gk->dcghkp", bb.reshape(g5), eye)
        return t.reshape(2, S5_CLUSTERS, CL_CH, CL_STATES)

    def out_blk(cc):
        t = jnp.einsum("dcghp,gk->dcgpkh", cc.reshape(g5), eye)
        return t.reshape(2, S5_CLUSTERS, CL_STATES, CL_CH)

    b_blk = jnp.concatenate([in_blk(bb_re), in_blk(bb_im)], axis=-1).astype(BF16)
    c_blk = jnp.concatenate([out_blk(c_re), out_blk(-c_im)], axis=-2).astype(BF16)
    a_vec = jnp.stack([ab_re[:, :, 0, :], ab_im[:, :, 0, :]], axis=0)
    a_vec = a_vec.reshape(2, 2, S5_CLUSTERS, SUBLANES, LANES)
    a_vec = jnp.transpose(a_vec, (1, 2, 0, 3, 4)).reshape(12, SUBLANES, LANES)
    return b_blk, c_blk, a_vec


def _pack_state(s_re, s_im):
    b = s_re.shape[0]
    st = jnp.stack([s_re, s_im], axis=2)
    st = st.reshape(b, 2, 2, S5_CLUSTERS, SUBLANES, LANES)
    return jnp.transpose(st, (0, 1, 3, 2, 4, 5)).reshape(b, 12, SUBLANES, LANES)


def _unpack_state(st):
    b = st.shape[0]
    st = st.reshape(b, 2, S5_CLUSTERS, 2, SUBLANES, LANES)
    st = jnp.transpose(st, (0, 3, 1, 2, 4, 5)).reshape(b, 2, 2, S5_GROUPS, S5_STATE)
    return st[:, 0], st[:, 1]


def _s5_kernel(uf_ref, ub_ref, h0_ref, a_ref, bblk_ref, cblk_ref, yf_ref, yb_ref, hf_ref,
               s_ref, st_ref, *, tt, pitch):
    j = pl.program_id(1)
    nj = pl.num_programs(1)
    nsl = 2 * SUBLANES

    @pl.when(j == 0)
    def _():
        st_ref[...] = h0_ref[0]

    for d, u_ref in enumerate((uf_ref, ub_ref)):
        for c in range(S5_CLUSTERS):
            k = d * S5_CLUSTERS + c
            u = u_ref[:, c * CL_CH:(c + 1) * CL_CH].astype(BF16)
            res = jnp.dot(u, bblk_ref[d, c], preferred_element_type=F32)
            for q in range(nsl):
                s_ref[k, pl.ds(q * pitch, tt), :] = res[:, q * LANES:(q + 1) * LANES]

    a = [a_ref[i] for i in range(12)]
    im0 = SUBLANES * pitch

    def body(t, carry):
        new = []
        for k in range(2 * S5_CLUSTERS):
            row = t if k < S5_CLUSTERS else tt - 1 - t
            hr, hi = carry[2 * k], carry[2 * k + 1]
            ar, ai = a[2 * k], a[2 * k + 1]
            br = s_ref[k, pl.ds(row, SUBLANES, stride=pitch), :]
            bi = s_ref[k, pl.ds(im0 + row, SUBLANES, stride=pitch), :]
            nr = ar * hr - ai * hi + br
            ni = ar * hi + ai * hr + bi
            s_ref[k, pl.ds(row, SUBLANES, stride=pitch), :] = nr
            s_ref[k, pl.ds(im0 + row, SUBLANES, stride=pitch), :] = ni
            new += [nr, ni]
        return tuple(new)

    fin = lax.fori_loop(0, tt, body, tuple(st_ref[i] for i in range(12)), unroll=2)
    for i in range(12):
        st_ref[i] = fin[i]

    for d, y_ref in enumerate((yf_ref, yb_ref)):
        for c in range(S5_CLUSTERS):
            k = d * S5_CLUSTERS + c
            hcat = jnp.concatenate(
                [s_ref[k, pl.ds(q * pitch, tt), :].astype(BF16) for q in range(nsl)], axis=-1)
            y_ref[:, c * CL_CH:(c + 1) * CL_CH] = jnp.dot(
                hcat, cblk_ref[d, c], preferred_element_type=F32)

    @pl.when(j == nj - 1)
    def _():
        hf_ref[0] = st_ref[...]


def _s5_scan(z, n_seq, seq_len, h0, a_vec, b_blk, c_blk):
    tt, pitch = S5_TT, S5_PITCH
    nj = seq_len // tt
    n = n_seq * seq_len
    st_spec = pl.BlockSpec((1, 12, SUBLANES, LANES), lambda s, j: (s, 0, 0, 0))
    return pl.pallas_call(
        functools.partial(_s5_kernel, tt=tt, pitch=pitch),
        out_shape=[jax.ShapeDtypeStruct((n, S5_CH), F32),
                   jax.ShapeDtypeStruct((n, S5_CH), F32),
                   jax.ShapeDtypeStruct((n_seq, 12, SUBLANES, LANES), F32)],
        grid=(n_seq, nj),
        in_specs=[pl.BlockSpec((tt, S5_CH), lambda s, j: (s * nj + j, 0)),
                  pl.BlockSpec((tt, S5_CH), lambda s, j: (s * nj + nj - 1 - j, 0)),
                  st_spec,
                  pl.BlockSpec((12, SUBLANES, LANES), lambda s, j: (0, 0, 0)),
                  pl.BlockSpec(b_blk.shape, lambda s, j: (0, 0, 0, 0)),
                  pl.BlockSpec(c_blk.shape, lambda s, j: (0, 0, 0, 0))],
        out_specs=[pl.BlockSpec((tt, S5_CH), lambda s, j: (s * nj + j, 0)),
                   pl.BlockSpec((tt, S5_CH), lambda s, j: (s * nj + nj - 1 - j, 0)),
                   st_spec],
        scratch_shapes=[pltpu.VMEM((2 * S5_CLUSTERS, 2 * SUBLANES * pitch, LANES), F32),
                        pltpu.VMEM((12, SUBLANES, LANES), F32)],
        compiler_params=_cparams(("arbitrary", "arbitrary")),
        name="s5_scan",
    )(z, z, h0, a_vec, b_blk, c_blk)


def _s5_glu_kernel(yf_ref, yb_ref, u_ref, d_ref, w_ref, b_ref, o_ref):
    y = yf_ref[...] + yb_ref[...] + d_ref[...] * u_ref[...]
    g = jax.nn.gelu(y)
    gate = jnp.dot(g.astype(BF16), w_ref[...], preferred_element_type=F32) + b_ref[...]
    o_ref[...] = g * jax.nn.sigmoid(gate)


def _s5_glu(yf, yb, z, d, w_glu, b_glu, tm=512):
    n = yf.shape[0]
    row = pl.BlockSpec((tm, S5_CH), lambda i: (i, 0))
    vec = pl.BlockSpec((1, S5_CH), lambda i: (0, 0))
    return pl.pallas_call(
        _s5_glu_kernel,
        out_shape=jax.ShapeDtypeStruct((n, S5_CH), F32),
        grid=(n // tm,),
        in_specs=[row, row, row, vec, pl.BlockSpec((S5_CH, S5_CH), lambda i: (0, 0)), vec],
        out_specs=row,
        compiler_params=_cparams(("parallel",)),
        name="s5_glu",
    )(yf, yb, z, d.reshape(1, S5_CH), w_glu, b_glu.reshape(1, S5_CH))


def _fnet_kernel(u_ref, wf_ref, bf_ref, o_ref, cs_ref, cb_ref, sb_ref, vc_ref, vs_ref,
                 *, seq_len, tm, kc):
    s = pl.program_id(0)
    i = pl.program_id(1)
    two_pi = 2.0 * math.pi

    @pl.when(jnp.logical_and(s == 0, i == 0))
    def _():
        r = lax.broadcasted_iota(I32, (FNET_CH, FNET_CH), 0)
        k = lax.broadcasted_iota(I32, (FNET_CH, FNET_CH), 1)
        same = (r >> 6) == (k >> 6)
        ang = (((r & 63) * (k & 63)) & 63).astype(F32) * (two_pi / FNET_GW)
        cs_ref[:, 0:FNET_CH] = jnp.where(same, jnp.cos(ang), 0.0).astype(BF16)
        cs_ref[:, FNET_CH:] = jnp.where(same, jnp.sin(ang), 0.0).astype(BF16)

        def tab(rb, carry):
            r0 = pl.multiple_of(rb * SUBLANES, SUBLANES)
            j0 = lax.broadcasted_iota(I32, (SUBLANES, seq_len), 0) + r0
            kk = lax.broadcasted_iota(I32, (SUBLANES, seq_len), 1)
            a = ((j0 * kk) & (seq_len - 1)).astype(F32) * (two_pi / seq_len)
            cb_ref[pl.ds(r0, SUBLANES), :] = jnp.cos(a)
            sb_ref[pl.ds(r0, SUBLANES), :] = jnp.sin(a)
            return carry

        lax.fori_loop(0, tm // SUBLANES, tab, 0)

    @pl.when(i == 0)
    def _():
        rc = min(seq_len, 512)
        for r in range(seq_len // rc):
            sl = slice(r * rc, (r + 1) * rc)
            v = jnp.dot(u_ref[sl, :].astype(BF16), cs_ref[...], preferred_element_type=F32)
            vc_ref[sl, :] = v[:, 0:FNET_CH].astype(BF16)
            vs_ref[sl, :] = v[:, FNET_CH:].astype(BF16)

    kk = lax.broadcasted_iota(I32, (1, seq_len), 1)
    ang = (((i * tm) * kk) & (seq_len - 1)).astype(F32) * (two_pi / seq_len)
    ca = jnp.cos(ang)
    sa = jnp.sin(ang)
    acc = jnp.zeros((tm, FNET_CH), F32)
    for kci in range(seq_len // kc):
        sl = slice(kci * kc, (kci + 1) * kc)
        cb = cb_ref[:, sl]
        sb = sb_ref[:, sl]
        cmat = (ca[:, sl] * cb - sa[:, sl] * sb).astype(BF16)
        smat = (sa[:, sl] * cb + ca[:, sl] * sb).astype(BF16)
        acc = acc + jnp.dot(cmat, vc_ref[sl, :], preferred_element_type=F32)
        acc = acc - jnp.dot(smat, vs_ref[sl, :], preferred_element_type=F32)
    zf = acc * (1.0 / math.sqrt(seq_len * FNET_GW))
    o_ref[...] = jnp.dot(zf.astype(BF16), wf_ref[...], preferred_element_type=F32) + bf_ref[...]


def _fnet(z, n_seq, seq_len, w_blk, b_vec):
    tm = 256
    kc = min(seq_len, 1024)
    ni = seq_len // tm
    n = n_seq * seq_len
    col = (D - FNET_CH) // FNET_CH
    return pl.pallas_call(
        functools.partial(_fnet_kernel, seq_len=seq_len, tm=tm, kc=kc),
        out_shape=jax.ShapeDtypeStruct((n, FNET_CH), F32),
        grid=(n_seq, ni),
        in_specs=[pl.BlockSpec((seq_len, FNET_CH), lambda s, i: (s, col)),
                  pl.BlockSpec((FNET_CH, FNET_CH), lambda s, i: (0, 0)),
                  pl.BlockSpec((1, FNET_CH), lambda s, i: (0, 0))],
        out_specs=pl.BlockSpec((tm, FNET_CH), lambda s, i: (s * ni + i, 0)),
        scratch_shapes=[pltpu.VMEM((FNET_CH, 2 * FNET_CH), BF16),
                        pltpu.VMEM((tm, seq_len), F32),
                        pltpu.VMEM((tm, seq_len), F32),
                        pltpu.VMEM((seq_len, FNET_CH), BF16),
                        pltpu.VMEM((seq_len, FNET_CH), BF16)],
        compiler_params=_cparams(("arbitrary", "arbitrary")),
        name="fnet",
    )(z, w_blk, b_vec)


def _odd_mixer_kernel(z_ref, prev_ref, next_ref, pw_ref, ps_ref, lng_ref, lnb_ref, ws_ref, bs_ref,
                      o_ref, xe_ref, *, tt, seq_len):
    j = pl.program_id(1)
    nj = pl.num_programs(1)
    h = POOL_HALO

    xe_ref[0:h, :] = jnp.where(j > 0, prev_ref[...], 0.0)
    xe_ref[h:h + tt, :] = z_ref[:, 0:POOL_CH]
    xe_ref[h + tt:h + tt + h, :] = jnp.where(j < nj - 1, next_ref[...], 0.0)

    t = j * tt + lax.broadcasted_iota(I32, (tt, 1), 0)
    for gi, win in enumerate(POOL_WINDOWS):
        half = win // 2
        cols = slice(gi * POOL_GW, (gi + 1) * POOL_GW)
        acc = xe_ref[pl.ds(h - half, tt), cols]
        for o in range(-half + 1, half):
            acc = acc + xe_ref[pl.ds(h + o, tt), cols]
        cnt = (jnp.minimum(t + half, seq_len) - jnp.maximum(t - half, 0)).astype(F32)
        p = acc / cnt - z_ref[:, cols]
        mixed = jnp.dot(p.astype(BF16), pw_ref[gi], preferred_element_type=F32)
        o_ref[:, cols] = mixed * ps_ref[:, cols]

    for hd in range(GMLP_HEADS):
        ucols = slice(POOL_CH + hd * GMLP_HD, POOL_CH + (hd + 1) * GMLP_HD)
        vcols = slice(POOL_CH + GMLP_CH + hd * GMLP_HD, POOL_CH + GMLP_CH + (hd + 1) * GMLP_HD)
        pcols = slice(hd * GMLP_HD, (hd + 1) * GMLP_HD)
        u = jax.nn.gelu(z_ref[:, ucols])
        v = jax.nn.gelu(z_ref[:, vcols])
        mu = jnp.mean(v, axis=-1, keepdims=True)
        vc = v - mu
        var = jnp.mean(vc * vc, axis=-1, keepdims=True)
        vn = (vc * lax.rsqrt(var + EPS) * lng_ref[:, pcols] + lnb_ref[:, pcols]).astype(BF16)
        for ch in range(tt // GMLP_CHUNK):
            rows = slice(ch * GMLP_CHUNK, (ch + 1) * GMLP_CHUNK)
            sv = jnp.dot(ws_ref[hd], vn[rows, :], preferred_element_type=F32) + bs_ref[:, hd:hd + 1]
            o_ref[rows, POOL_CH + hd * GMLP_HD:POOL_CH + (hd + 1) * GMLP_HD] = u[rows, :] * sv


def _odd_mixer(z, n_seq, seq_len, pool_w, pool_scale, ln_g, ln_b, ws, bs_t, tt):
    nj = seq_len // tt
    n = n_seq * seq_len
    hb = tt // POOL_HALO
    nhb = seq_len // POOL_HALO
    width = z.shape[1]
    return pl.pallas_call(
        functools.partial(_odd_mixer_kernel, tt=tt, seq_len=seq_len),
        out_shape=jax.ShapeDtypeStruct((n, D), F32),
        grid=(n_seq, nj),
        in_specs=[pl.BlockSpec((tt, width), lambda s, j: (s * nj + j, 0)),
                  pl.BlockSpec((POOL_HALO, POOL_CH),
                               lambda s, j: (s * nhb + jnp.maximum(j * hb - 1, 0), 0)),
                  pl.BlockSpec((POOL_HALO, POOL_CH),
                               lambda s, j: (s * nhb + jnp.minimum((j + 1) * hb, nhb - 1), 0)),
                  pl.BlockSpec(pool_w.shape, lambda s, j: (0, 0, 0)),
                  pl.BlockSpec((1, POOL_CH), lambda s, j: (0, 0)),
                  pl.BlockSpec((1, GMLP_CH), lambda s, j: (0, 0)),
                  pl.BlockSpec((1, GMLP_CH), lambda s, j: (0, 0)),
                  pl.BlockSpec(ws.shape, lambda s, j: (0, 0, 0)),
                  pl.BlockSpec(bs_t.shape, lambda s, j: (0, 0))],
        out_specs=pl.BlockSpec((tt, D), lambda s, j: (s * nj + j, 0)),
        scratch_shapes=[pltpu.VMEM((tt + 2 * POOL_HALO, POOL_CH), F32)],
        compiler_params=_cparams(("parallel", "parallel")),
        name="odd_mixer",
    )(z, z, z, pool_w, pool_scale.reshape(1, POOL_CH), ln_g.reshape(1, GMLP_CH),
      ln_b.reshape(1, GMLP_CH), ws, bs_t)


def _trunk(x, n_seq, seq_len, cond_map, mods0, mods1, h0, p):
    z = _inproj(x, p["l0_g_mix_pre"], mods0, cond_map, p["l0_w_in"])
    yf, yb, hfin = _s5_scan(z, n_seq, seq_len, h0, p["a_vec"], p["b_blk"], p["c_blk"])
    ya = _s5_glu(yf, yb, z, p["l0_s5_d"], p["l0_s5_w_glu"], p["l0_s5_b_glu"])
    yfn = _fnet(z, n_seq, seq_len, p["fnet_w_blk"], p["fnet_b"])
    x = _outproj([ya, yfn], [p["l0_w_out_a"], p["l0_w_out_b"]], x, p["l0_g_mix_post"], mods0, cond_map)
    x = _ffn(x, p["l0_g_ff_pre"], p["l0_g_ff_post"], mods0, cond_map, p["l0_w_ff1"], p["l0_w_ff2"])
    z = _inproj(x, p["l1_g_mix_pre"], mods1, cond_map, p["l1_w_in"])
    y = _odd_mixer(z, n_seq, seq_len, p["pool_w"], p["l1_pool_scale"], p["l1_gmlp_ln_g"],
                   p["l1_gmlp_ln_b"], p["gmlp_ws"], p["gmlp_bs_t"], tt=min(seq_len, 512))
    x = _outproj([y], [p["l1_w_out"]], x, p["l1_g_mix_post"], mods1, cond_map)
    x = _ffn(x, p["l1_g_ff_pre"], p["l1_g_ff_post"], mods1, cond_map, p["l1_w_ff1"], p["l1_w_ff2"])
    return x, hfin


def kernel(x_prompt, x_sample, state_l0_s5_re, state_l0_s5_im, c, c_ctx, l0_w_mod, l0_b_mod, l0_g_mix_pre, l0_g_mix_post, l0_g_ff_pre, l0_g_ff_post, l0_w_ff1, l0_w_ff2, l0_w_in, l0_w_out, l0_s5_lambda_re, l0_s5_lambda_im, l0_s5_log_dt, l0_s5_b_re, l0_s5_b_im, l0_s5_c_re, l0_s5_c_im, l0_s5_d, l0_s5_w_glu, l0_s5_b_glu, l0_fnet_w, l0_fnet_b, l1_w_mod, l1_b_mod, l1_g_mix_pre, l1_g_mix_post, l1_g_ff_pre, l1_g_ff_post, l1_w_ff1, l1_w_ff2, l1_w_in, l1_w_out, l1_pool_w, l1_pool_scale, l1_gmlp_ln_g, l1_gmlp_ln_b, l1_gmlp_ws, l1_gmlp_bs):
    batch, seq, _ = x_prompt.shape
    dec_batch, dec_seq, _ = x_sample.shape
    assert dec_seq == GRID_W * GRID_W and seq % S5_TT == 0

    conds = jnp.zeros((8, D), F32).at[0].set(c_ctx).at[1:1 + dec_batch].set(c)
    mods0 = _modulation(conds, l0_w_mod, l0_b_mod)
    mods1 = _modulation(conds, l1_w_mod, l1_b_mod)

    ab_re, ab_im, bb_re, bb_im = _s5_discretize(l0_s5_lambda_re, l0_s5_lambda_im, l0_s5_log_dt,
                                                l0_s5_b_re, l0_s5_b_im)
    b_blk, c_blk, a_vec = _s5_block_params(ab_re, ab_im, bb_re, bb_im, l0_s5_c_re, l0_s5_c_im)

    eye4 = jnp.eye(FNET_CH // FNET_GW, dtype=F32)
    fnet_w_blk = jnp.einsum("gcd,gk->gckd", l0_fnet_w, eye4).reshape(FNET_CH, FNET_CH).astype(BF16)

    p = dict(
        l0_g_mix_pre=l0_g_mix_pre, l0_g_mix_post=l0_g_mix_post, l0_g_ff_pre=l0_g_ff_pre,
        l0_g_ff_post=l0_g_ff_post, l0_w_ff1=l0_w_ff1.astype(BF16), l0_w_ff2=l0_w_ff2.astype(BF16),
        l0_w_in=l0_w_in.astype(BF16), l0_w_out_a=l0_w_out[:S5_CH].astype(BF16),
        l0_w_out_b=l0_w_out[S5_CH:].astype(BF16), l0_s5_d=l0_s5_d,
        l0_s5_w_glu=l0_s5_w_glu.astype(BF16), l0_s5_b_glu=l0_s5_b_glu,
        a_vec=a_vec, b_blk=b_blk, c_blk=c_blk, fnet_w_blk=fnet_w_blk,
        fnet_b=l0_fnet_b.reshape(1, FNET_CH),
        l1_g_mix_pre=l1_g_mix_pre, l1_g_mix_post=l1_g_mix_post, l1_g_ff_pre=l1_g_ff_pre,
        l1_g_ff_post=l1_g_ff_post, l1_w_ff1=l1_w_ff1.astype(BF16), l1_w_ff2=l1_w_ff2.astype(BF16),
        l1_w_in=l1_w_in.astype(BF16), l1_w_out=l1_w_out.astype(BF16),
        pool_w=l1_pool_w.astype(BF16), l1_pool_scale=l1_pool_scale, l1_gmlp_ln_g=l1_gmlp_ln_g,
        l1_gmlp_ln_b=l1_gmlp_ln_b, gmlp_ws=l1_gmlp_ws.astype(BF16), gmlp_bs_t=l1_gmlp_bs.T,
    )

    zero_state = jnp.zeros((batch, 12, SUBLANES, LANES), F32)
    y_prompt, hfin = _trunk(x_prompt.reshape(batch * seq, D), batch, seq, lambda r: 0,
                            mods0, mods1, zero_state, p)
    new_re, new_im = _unpack_state(hfin)

    xs = _embed(x_sample.reshape(dec_batch * dec_seq, D))
    h0 = _pack_state(state_l0_s5_re, state_l0_s5_im)
    y_sample, _ = _trunk(xs, dec_batch, dec_seq, lambda r: 1 + r // dec_seq, mods0, mods1, h0, p)

    return (y_prompt.reshape(batch, seq, D), y_sample.reshape(dec_batch, dec_seq, D), new_re, new_im)
```

```python
import functools
import math

import jax
import jax.numpy as jnp
import numpy as np
from jax import lax
from jax.experimental import pallas as pl
from jax.experimental.pallas import tpu as pltpu

F32 = jnp.float32
BF16 = jnp.bfloat16
I32 = jnp.int32

D = 1024
D_FF = 4 * D
N_MOD = 6
EPS = 1e-6
GRID_W = 64
POS_BASE = 10000.0

S5_CH = 768
S5_GROUPS = 48
S5_GROUP = 16
S5_STATE = 64
S5_CLUSTERS = 3
CL_CH = 256
CL_STATES = 1024
FNET_CH = 256
FNET_GW = 64

POOL_WINDOWS = (2, 4, 8, 16)
POOL_CH = 512
POOL_GW = 128
POOL_HALO = 8
GMLP_CH = 512
GMLP_HEADS = 4
GMLP_HD = 128
GMLP_CHUNK = 128

LANES = 128
SUBLANES = 8
VMEM_LIMIT = 56 * 1024 * 1024

S5_TT = 128


def _cparams(sem):
    return pltpu.CompilerParams(dimension_semantics=sem, vmem_limit_bytes=VMEM_LIMIT)


def _rms(x, g):
    return x * lax.rsqrt(jnp.mean(x * x, axis=-1, keepdims=True) + EPS) * g


def _bdot(a, b):
    return jnp.dot(a.astype(BF16), b.astype(BF16), preferred_element_type=F32)


def _mod_kernel(c_ref, w_ref, b_ref, o_ref):
    c = c_ref[...]
    s = c * jax.nn.sigmoid(c)
    o_ref[...] = _bdot(s, w_ref[...]) + b_ref[...]


def _modulation(conds, w_mod, b_mod):
    tn = 1024
    out = pl.pallas_call(
        _mod_kernel,
        out_shape=jax.ShapeDtypeStruct((8, N_MOD * D), F32),
        grid=(N_MOD * D // tn,),
        in_specs=[pl.BlockSpec((8, D), lambda j: (0, 0)),
                  pl.BlockSpec((D, tn), lambda j: (0, j)),
                  pl.BlockSpec((1, tn), lambda j: (0, j))],
        out_specs=pl.BlockSpec((8, tn), lambda j: (0, j)),
        compiler_params=_cparams(("parallel",)),
        name="modulation",
    )(conds, w_mod, b_mod.reshape(1, -1))
    return out.reshape(8, 1, N_MOD * D)


def _embed_kernel(x_ref, o_ref, tab_ref, *, tm):
    i = pl.program_id(0)
    quarter = D // 4
    k = lax.broadcasted_iota(I32, (GRID_W, quarter), 1).astype(F32)
    p = lax.broadcasted_iota(I32, (GRID_W, quarter), 0).astype(F32)
    omega = jnp.exp(k * (-math.log(POS_BASE) / quarter))
    ang = p * omega
    tab_ref[:, 0:quarter] = jnp.sin(ang)
    tab_ref[:, quarter:2 * quarter] = jnp.cos(ang)
    rows_per_seq = 4096 // GRID_W
    row0 = (i * (tm // GRID_W)) % rows_per_seq
    for r in range(tm // GRID_W):
        sl = slice(r * GRID_W, (r + 1) * GRID_W)
        rowvec = tab_ref[pl.ds(row0 + r, 1), :]
        o_ref[sl, 0:2 * quarter] = x_ref[sl, 0:2 * quarter] + rowvec
        o_ref[sl, 2 * quarter:] = x_ref[sl, 2 * quarter:] + tab_ref[...]


def _embed(x, tm=512):
    n = x.shape[0]
    return pl.pallas_call(
        functools.partial(_embed_kernel, tm=tm),
        out_shape=jax.ShapeDtypeStruct((n, D), F32),
        grid=(n // tm,),
        in_specs=[pl.BlockSpec((tm, D), lambda i: (i, 0))],
        out_specs=pl.BlockSpec((tm, D), lambda i: (i, 0)),
        scratch_shapes=[pltpu.VMEM((GRID_W, D // 2), F32)],
        compiler_params=_cparams(("parallel",)),
        name="pos_embed",
    )(x)


def _inproj_kernel(x_ref, g_ref, m_ref, w_ref, o_ref, *, off):
    m = m_ref[0]
    sh = m[:, off * D:(off + 1) * D]
    sc = m[:, (off + 1) * D:(off + 2) * D]
    h = _rms(x_ref[...], g_ref[...]) * (1.0 + sc) + sh
    o_ref[...] = jnp.dot(h.astype(BF16), w_ref[...], preferred_element_type=F32)


def _inproj(x, g, mods, cond_map, w_bf16, tm=512):
    n = x.shape[0]
    nout = w_bf16.shape[1]
    return pl.pallas_call(
        functools.partial(_inproj_kernel, off=0),
        out_shape=jax.ShapeDtypeStruct((n, nout), F32),
        grid=(n // tm,),
        in_specs=[pl.BlockSpec((tm, D), lambda i: (i, 0)),
                  pl.BlockSpec((1, D), lambda i: (0, 0)),
                  pl.BlockSpec((1, 1, N_MOD * D), lambda i: (cond_map(i * tm), 0, 0)),
                  pl.BlockSpec((D, nout), lambda i: (0, 0))],
        out_specs=pl.BlockSpec((tm, nout), lambda i: (i, 0)),
        compiler_params=_cparams(("parallel",)),
        name="inproj",
    )(x, g.reshape(1, D), mods, w_bf16)


def _outproj_kernel(*refs, n_in, gate_off):
    ins = refs[:n_in]
    ws = refs[n_in:2 * n_in]
    x_ref, g_ref, m_ref, o_ref = refs[2 * n_in:]
    y = None
    for a_ref, w_ref in zip(ins, ws):
        t = jnp.dot(a_ref[...].astype(BF16), w_ref[...], preferred_element_type=F32)
        y = t if y is None else y + t
    gt = m_ref[0][:, gate_off * D:(gate_off + 1) * D]
    o_ref[...] = x_ref[...] + gt * _rms(y, g_ref[...])


def _outproj(parts, weights, x, g, mods, cond_map, tm=512):
    n = x.shape[0]
    n_in = len(parts)
    in_specs = [pl.BlockSpec((tm, p.shape[1]), lambda i: (i, 0)) for p in parts]
    in_specs += [pl.BlockSpec(w.shape, lambda i: (0, 0)) for w in weights]
    in_specs += [pl.BlockSpec((tm, D), lambda i: (i, 0)),
                 pl.BlockSpec((1, D), lambda i: (0, 0)),
                 pl.BlockSpec((1, 1, N_MOD * D), lambda i: (cond_map(i * tm), 0, 0))]
    return pl.pallas_call(
        functools.partial(_outproj_kernel, n_in=n_in, gate_off=2),
        out_shape=jax.ShapeDtypeStruct((n, D), F32),
        grid=(n // tm,),
        in_specs=in_specs,
        out_specs=pl.BlockSpec((tm, D), lambda i: (i, 0)),
        compiler_params=_cparams(("parallel",)),
        name="outproj",
    )(*parts, *weights, x, g.reshape(1, D), mods)


def _ffn_kernel(x_ref, gpre_ref, gpost_ref, m_ref, w1_ref, w2_ref, o_ref, a_ref, *, n_chunks):
    m = m_ref[0]
    sh = m[:, 3 * D:4 * D]
    sc = m[:, 4 * D:5 * D]
    gt = m[:, 5 * D:6 * D]
    x = x_ref[...]
    h = (_rms(x, gpre_ref[...]) * (1.0 + sc) + sh).astype(BF16)
    cw = D_FF // n_chunks
    for f in range(n_chunks):
        cols = slice(f * cw, (f + 1) * cw)
        a = jnp.dot(h, w1_ref[:, cols], preferred_element_type=F32)
        a_ref[:, cols] = jnp.square(jnp.maximum(a, 0.0)).astype(BF16)
    y = jnp.dot(a_ref[...], w2_ref[...], preferred_element_type=F32)
    o_ref[...] = x + gt * _rms(y, gpost_ref[...])


def _resident(shape):
    nd = len(shape)
    return pl.BlockSpec(shape, lambda *_: (0,) * nd, pipeline_mode=pl.Buffered(1))


def _ffn(x, gpre, gpost, mods, cond_map, w1, w2, tm=512, n_chunks=4):
    n = x.shape[0]
    return pl.pallas_call(
        functools.partial(_ffn_kernel, n_chunks=n_chunks),
        out_shape=jax.ShapeDtypeStruct((n, D), F32),
        grid=(n // tm,),
        in_specs=[pl.BlockSpec((tm, D), lambda i: (i, 0)),
                  pl.BlockSpec((1, D), lambda i: (0, 0)),
                  pl.BlockSpec((1, D), lambda i: (0, 0)),
                  pl.BlockSpec((1, 1, N_MOD * D), lambda i: (cond_map(i * tm), 0, 0)),
                  _resident((D, D_FF)),
                  _resident((D_FF, D))],
        out_specs=pl.BlockSpec((tm, D), lambda i: (i, 0)),
        scratch_shapes=[pltpu.VMEM((tm, D_FF), BF16)],
        compiler_params=_cparams(("parallel",)),
        name="ffn",
    )(x, gpre.reshape(1, D), gpost.reshape(1, D), mods, w1, w2)


def _s5_disc_kernel(lre_ref, lim_ref, ldt_ref, bre_ref, bim_ref, are_ref, aim_ref, bbre_ref, bbim_ref):
    lam_re = lre_ref[...]
    lam_im = lim_ref[...]
    dt = jnp.exp(ldt_ref[...])
    mag = jnp.exp(lam_re * dt)
    ab_re = mag * jnp.cos(lam_im * dt)
    ab_im = mag * jnp.sin(lam_im * dt)
    num_re = ab_re - 1.0
    num_im = ab_im
    den = lam_re * lam_re + lam_im * lam_im
    coef_re = (num_re * lam_re + num_im * lam_im) / den
    coef_im = (num_im * lam_re - num_re * lam_im) / den
    b_re = bre_ref[...]
    b_im = bim_ref[...]
    are_ref[...] = ab_re
    aim_ref[...] = ab_im
    bbre_ref[...] = coef_re * b_re - coef_im * b_im
    bbim_ref[...] = coef_re * b_im + coef_im * b_re


def _s5_discretize(lam_re, lam_im, log_dt, b_re, b_im):
    rows = 2 * S5_GROUPS * S5_GROUP
    shp = (2, S5_GROUPS, S5_GROUP, S5_STATE)
    rep = lambda a: jnp.broadcast_to(a[:, :, None, :], shp).reshape(rows, S5_STATE)
    ldt = jnp.broadcast_to(log_dt[:, :, None, None], shp).reshape(rows, S5_STATE)
    tr = lambda b: jnp.swapaxes(b, 2, 3).reshape(rows, S5_STATE)
    spec = pl.BlockSpec((rows, S5_STATE), lambda: (0, 0))
    outs = pl.pallas_call(
        _s5_disc_kernel,
        out_shape=[jax.ShapeDtypeStruct((rows, S5_STATE), F32)] * 4,
        in_specs=[spec] * 5,
        out_specs=[spec] * 4,
        name="s5_discretize",
    )(rep(lam_re), rep(lam_im), ldt, tr(b_re), tr(b_im))
    return [o.reshape(shp) for o in outs]


def _s5_block_params(ab_re, ab_im, bb_re, bb_im, c_re, c_im):
    eye = jnp.eye(S5_GROUP, dtype=F32)
    g5 = (2, S5_CLUSTERS, 16, S5_GROUP, S5_STATE)

    def in_blk(bb):
        t = jnp.einsum("dcghp,gk->dcghkp", bb.reshape(g5), eye)
        return t.reshape(2, S5_CLUSTERS, CL_CH, CL_STATES)

    def out_blk(cc):
        t = jnp.einsum("dcghp,gk->dcgpkh", cc.reshape(g5), eye)
        return t.reshape(2, S5_CLUSTERS, CL_STATES, CL_CH)

    b_blk = jnp.concatenate([in_blk(bb_re), in_blk(bb_im)], axis=-1).astype(BF16)
    c_blk = jnp.concatenate([out_blk(c_re), out_blk(-c_im)], axis=-2).astype(BF16)
    a_vec = jnp.stack([ab_re[:, :, 0, :], ab_im[:, :, 0, :]], axis=0)
    a_vec = a_vec.reshape(2, 2, S5_CLUSTERS, SUBLANES, LANES)
    a_vec = jnp.transpose(a_vec, (1, 2, 0, 3, 4)).reshape(12, SUBLANES, LANES)
    return b_blk, c_blk, a_vec


def _pack_state(s_re, s_im):
    b = s_re.shape[0]
    st = jnp.stack([s_re, s_im], axis=2)
    st = st.reshape(b, 2, 2, S5_CLUSTERS, SUBLANES, LANES)
    return jnp.transpose(st, (0, 1, 3, 2, 4, 5)).reshape(b, 12, SUBLANES, LANES)


def _unpack_state(st):
    b = st.shape[0]
    st = st.reshape(b, 2, S5_CLUSTERS, 2, SUBLANES, LANES)
    st = jnp.transpose(st, (0, 3, 1, 2, 4, 5)).reshape(b, 2, 2, S5_GROUPS, S5_STATE)
    return st[:, 0], st[:, 1]


def _s5_kernel(uf_ref, ub_ref, h0_ref, a_ref, bblk_ref, cblk_ref, yf_ref, yb_ref, hf_ref,
               s_ref, h_ref, st_ref, *, tt):
    j = pl.program_id(1)
    nj = pl.num_programs(1)
    im0 = tt * SUBLANES

    @pl.when(j == 0)
    def _():
        st_ref[...] = h0_ref[0]

    for d, u_ref in enumerate((uf_ref, ub_ref)):
        for c in range(S5_CLUSTERS):
            k = d * S5_CLUSTERS + c
            u = u_ref[:, c * CL_CH:(c + 1) * CL_CH].astype(BF16)
            res = jnp.dot(u, bblk_ref[d, c], preferred_element_type=F32)
            for q in range(SUBLANES):
                s_ref[k, pl.ds(q, tt, stride=SUBLANES), :] = res[:, q * LANES:(q + 1) * LANES]
                s_ref[k, pl.ds(im0 + q, tt, stride=SUBLANES), :] = (
                    res[:, CL_STATES + q * LANES:CL_STATES + (q + 1) * LANES])

    a = [a_ref[i] for i in range(12)]

    def body(t, carry):
        new = []
        for k in range(2 * S5_CLUSTERS):
            row = pl.multiple_of((t if k < S5_CLUSTERS else tt - 1 - t) * SUBLANES, SUBLANES)
            hr, hi = carry[2 * k], carry[2 * k + 1]
            ar, ai = a[2 * k], a[2 * k + 1]
            br = s_ref[k, pl.ds(row, SUBLANES), :]
            bi = s_ref[k, pl.ds(im0 + row, SUBLANES), :]
            nr = ar * hr - ai * hi + br
            ni = ar * hi + ai * hr + bi
            h_ref[k, pl.ds(row, SUBLANES), :] = nr
            h_ref[k, pl.ds(im0 + row, SUBLANES), :] = ni
            new += [nr, ni]
        return tuple(new)

    fin = lax.fori_loop(0, tt, body, tuple(st_ref[i] for i in range(12)), unroll=4)
    for i in range(12):
        st_ref[i] = fin[i]

    for d, y_ref in enumerate((yf_ref, yb_ref)):
        for c in range(S5_CLUSTERS):
            k = d * S5_CLUSTERS + c
            hcat = jnp.concatenate(
                [h_ref[k, pl.ds(base + q, tt, stride=SUBLANES), :].astype(BF16)
                 for base in (0, im0) for q in range(SUBLANES)], axis=-1)
            y_ref[:, c * CL_CH:(c + 1) * CL_CH] = jnp.dot(
                hcat, cblk_ref[d, c], preferred_element_type=F32)

    @pl.when(j == nj - 1)
    def _():
        hf_ref[0] = st_ref[...]


def _s5_scan(z, n_seq, seq_len, h0, a_vec, b_blk, c_blk):
    tt = S5_TT
    nj = seq_len // tt
    n = n_seq * seq_len
    st_spec = pl.BlockSpec((1, 12, SUBLANES, LANES), lambda s, j: (s, 0, 0, 0))
    return pl.pallas_call(
        functools.partial(_s5_kernel, tt=tt),
        out_shape=[jax.ShapeDtypeStruct((n, S5_CH), F32),
                   jax.ShapeDtypeStruct((n, S5_CH), F32),
                   jax.ShapeDtypeStruct((n_seq, 12, SUBLANES, LANES), F32)],
        grid=(n_seq, nj),
        in_specs=[pl.BlockSpec((tt, S5_CH), lambda s, j: (s * nj + j, 0)),
                  pl.BlockSpec((tt, S5_CH), lambda s, j: (s * nj + nj - 1 - j, 0)),
                  st_spec,
                  pl.BlockSpec((12, SUBLANES, LANES), lambda s, j: (0, 0, 0)),
                  _resident(b_blk.shape),
                  _resident(c_blk.shape)],
        out_specs=[pl.BlockSpec((tt, S5_CH), lambda s, j: (s * nj + j, 0)),
                   pl.BlockSpec((tt, S5_CH), lambda s, j: (s * nj + nj - 1 - j, 0)),
                   st_spec],
        scratch_shapes=[pltpu.VMEM((2 * S5_CLUSTERS, 2 * SUBLANES * tt, LANES), F32),
                        pltpu.VMEM((2 * S5_CLUSTERS, 2 * SUBLANES * tt, LANES), F32),
                        pltpu.VMEM((12, SUBLANES, LANES), F32)],
        compiler_params=_cparams(("arbitrary", "arbitrary")),
        name="s5_scan",
    )(z, z, h0, a_vec, b_blk, c_blk)


def _s5_glu_kernel(yf_ref, yb_ref, u_ref, d_ref, w_ref, b_ref, o_ref):
    y = yf_ref[...] + yb_ref[...] + d_ref[...] * u_ref[...]
    g = jax.nn.gelu(y)
    gate = jnp.dot(g.astype(BF16), w_ref[...], preferred_element_type=F32) + b_ref[...]
    o_ref[...] = g * jax.nn.sigmoid(gate)


def _s5_glu(yf, yb, z, d, w_glu, b_glu, tm=512):
    n = yf.shape[0]
    row = pl.BlockSpec((tm, S5_CH), lambda i: (i, 0))
    vec = pl.BlockSpec((1, S5_CH), lambda i: (0, 0))
    return pl.pallas_call(
        _s5_glu_kernel,
        out_shape=jax.ShapeDtypeStruct((n, S5_CH), F32),
        grid=(n // tm,),
        in_specs=[row, row, row, vec, pl.BlockSpec((S5_CH, S5_CH), lambda i: (0, 0)), vec],
        out_specs=row,
        compiler_params=_cparams(("parallel",)),
        name="s5_glu",
    )(yf, yb, z, d.reshape(1, S5_CH), w_glu, b_glu.reshape(1, S5_CH))


def _fnet_kernel(u_ref, wf_ref, bf_ref, o_ref, cs_ref, cb_ref, sb_ref, vc_ref, vs_ref,
                 *, seq_len, tm, kc):
    s = pl.program_id(0)
    i = pl.program_id(1)
    two_pi = 2.0 * math.pi

    @pl.when(jnp.logical_and(s == 0, i == 0))
    def _():
        r = lax.broadcasted_iota(I32, (FNET_CH, FNET_CH), 0)
        k = lax.broadcasted_iota(I32, (FNET_CH, FNET_CH), 1)
        same = (r >> 6) == (k >> 6)
        ang = (((r & 63) * (k & 63)) & 63).astype(F32) * (two_pi / FNET_GW)
        cs_ref[:, 0:FNET_CH] = jnp.where(same, jnp.cos(ang), 0.0).astype(BF16)
        cs_ref[:, FNET_CH:] = jnp.where(same, jnp.sin(ang), 0.0).astype(BF16)

        def tab(rb, carry):
            r0 = pl.multiple_of(rb * SUBLANES, SUBLANES)
            j0 = lax.broadcasted_iota(I32, (SUBLANES, seq_len), 0) + r0
            kk = lax.broadcasted_iota(I32, (SUBLANES, seq_len), 1)
            a = ((j0 * kk) & (seq_len - 1)).astype(F32) * (two_pi / seq_len)
            cb_ref[pl.ds(r0, SUBLANES), :] = jnp.cos(a)
            sb_ref[pl.ds(r0, SUBLANES), :] = jnp.sin(a)
            return carry

        lax.fori_loop(0, tm // SUBLANES, tab, 0)

    @pl.when(i == 0)
    def _():
        rc = min(seq_len, 512)
        for r in range(seq_len // rc):
            sl = slice(r * rc, (r + 1) * rc)
            v = jnp.dot(u_ref[sl, :].astype(BF16), cs_ref[...], preferred_element_type=F32)
            vc_ref[sl, :] = v[:, 0:FNET_CH].astype(BF16)
            vs_ref[sl, :] = v[:, FNET_CH:].astype(BF16)

    kk = lax.broadcasted_iota(I32, (1, seq_len), 1)
    ang = (((i * tm) * kk) & (seq_len - 1)).astype(F32) * (two_pi / seq_len)
    ca = jnp.cos(ang)
    sa = jnp.sin(ang)
    acc = jnp.zeros((tm, FNET_CH), F32)
    for kci in range(seq_len // kc):
        sl = slice(kci * kc, (kci + 1) * kc)
        cb = cb_ref[:, sl]
        sb = sb_ref[:, sl]
        cmat = (ca[:, sl] * cb - sa[:, sl] * sb).astype(BF16)
        smat = (sa[:, sl] * cb + ca[:, sl] * sb).astype(BF16)
        acc = acc + jnp.dot(cmat, vc_ref[sl, :], preferred_element_type=F32)
        acc = acc - jnp.dot(smat, vs_ref[sl, :], preferred_element_type=F32)
    zf = acc * (1.0 / math.sqrt(seq_len * FNET_GW))
    o_ref[...] = jnp.dot(zf.astype(BF16), wf_ref[...], preferred_element_type=F32) + bf_ref[...]


def _fnet(z, n_seq, seq_len, w_blk, b_vec):
    tm = 256
    kc = min(seq_len, 1024)
    ni = seq_len // tm
    n = n_seq * seq_len
    col = (D - FNET_CH) // FNET_CH
    return pl.pallas_call(
        functools.partial(_fnet_kernel, seq_len=seq_len, tm=tm, kc=kc),
        out_shape=jax.ShapeDtypeStruct((n, FNET_CH), F32),
        grid=(n_seq, ni),
        in_specs=[pl.BlockSpec((seq_len, FNET_CH), lambda s, i: (s, col)),
                  pl.BlockSpec((FNET_CH, FNET_CH), lambda s, i: (0, 0)),
                  pl.BlockSpec((1, FNET_CH), lambda s, i: (0, 0))],
        out_specs=pl.BlockSpec((tm, FNET_CH), lambda s, i: (s * ni + i, 0)),
        scratch_shapes=[pltpu.VMEM((FNET_CH, 2 * FNET_CH), BF16),
                        pltpu.VMEM((tm, seq_len), F32),
                        pltpu.VMEM((tm, seq_len), F32),
                        pltpu.VMEM((seq_len, FNET_CH), BF16),
                        pltpu.VMEM((seq_len, FNET_CH), BF16)],
        compiler_params=_cparams(("arbitrary", "arbitrary")),
        name="fnet",
    )(z, w_blk, b_vec)


def _odd_mixer_kernel(z_ref, prev_ref, next_ref, pw_ref, ps_ref, lng_ref, lnb_ref, ws_ref, bs_ref,
                      o_ref, xe_ref, *, tt, seq_len):
    j = pl.program_id(1)
    nj = pl.num_programs(1)
    h = POOL_HALO

    xe_ref[0:h, :] = jnp.where(j > 0, prev_ref[...], 0.0)
    xe_ref[h:h + tt, :] = z_ref[:, 0:POOL_CH]
    xe_ref[h + tt:h + tt + h, :] = jnp.where(j < nj - 1, next_ref[...], 0.0)

    t = j * tt + lax.broadcasted_iota(I32, (tt, 1), 0)
    for gi, win in enumerate(POOL_WINDOWS):
        half = win // 2
        cols = slice(gi * POOL_GW, (gi + 1) * POOL_GW)
        acc = xe_ref[pl.ds(h - half, tt), cols]
        for o in range(-half + 1, half):
            acc = acc + xe_ref[pl.ds(h + o, tt), cols]
        cnt = (jnp.minimum(t + half, seq_len) - jnp.maximum(t - half, 0)).astype(F32)
        p = acc / cnt - z_ref[:, cols]
        mixed = jnp.dot(p.astype(BF16), pw_ref[gi], preferred_element_type=F32)
        o_ref[:, cols] = mixed * ps_ref[:, cols]

    for hd in range(GMLP_HEADS):
        ucols = slice(POOL_CH + hd * GMLP_HD, POOL_CH + (hd + 1) * GMLP_HD)
        vcols = slice(POOL_CH + GMLP_CH + hd * GMLP_HD, POOL_CH + GMLP_CH + (hd + 1) * GMLP_HD)
        pcols = slice(hd * GMLP_HD, (hd + 1) * GMLP_HD)
        u = jax.nn.gelu(z_ref[:, ucols])
        v = jax.nn.gelu(z_ref[:, vcols])
        mu = jnp.mean(v, axis=-1, keepdims=True)
        vc = v - mu
        var = jnp.mean(vc * vc, axis=-1, keepdims=True)
        vn = (vc * lax.rsqrt(var + EPS) * lng_ref[:, pcols] + lnb_ref[:, pcols]).astype(BF16)
        for ch in range(tt // GMLP_CHUNK):
            rows = slice(ch * GMLP_CHUNK, (ch + 1) * GMLP_CHUNK)
            sv = jnp.dot(ws_ref[hd], vn[rows, :], preferred_element_type=F32) + bs_ref[:, hd:hd + 1]
            o_ref[rows, POOL_CH + hd * GMLP_HD:POOL_CH + (hd + 1) * GMLP_HD] = u[rows, :] * sv


def _odd_mixer(z, n_seq, seq_len, pool_w, pool_scale, ln_g, ln_b, ws, bs_t, tt):
    nj = seq_len // tt
    n = n_seq * seq_len
    hb = tt // POOL_HALO
    nhb = seq_len // POOL_HALO
    width = z.shape[1]
    return pl.pallas_call(
        functools.partial(_odd_mixer_kernel, tt=tt, seq_len=seq_len),
        out_shape=jax.ShapeDtypeStruct((n, D), F32),
        grid=(n_seq, nj),
        in_specs=[pl.BlockSpec((tt, width), lambda s, j: (s * nj + j, 0)),
                  pl.BlockSpec((POOL_HALO, POOL_CH),
                               lambda s, j: (s * nhb + jnp.maximum(j * hb - 1, 0), 0)),
                  pl.BlockSpec((POOL_HALO, POOL_CH),
                               lambda s, j: (s * nhb + jnp.minimum((j + 1) * hb, nhb - 1), 0)),
                  pl.BlockSpec(pool_w.shape, lambda s, j: (0, 0, 0)),
                  pl.BlockSpec((1, POOL_CH), lambda s, j: (0, 0)),
                  pl.BlockSpec((1, GMLP_CH), lambda s, j: (0, 0)),
                  pl.BlockSpec((1, GMLP_CH), lambda s, j: (0, 0)),
                  pl.BlockSpec(ws.shape, lambda s, j: (0, 0, 0)),
                  pl.BlockSpec(bs_t.shape, lambda s, j: (0, 0))],
        out_specs=pl.BlockSpec((tt, D), lambda s, j: (s * nj + j, 0)),
        scratch_shapes=[pltpu.VMEM((tt + 2 * POOL_HALO, POOL_CH), F32)],
        compiler_params=_cparams(("parallel", "parallel")),
        name="odd_mixer",
    )(z, z, z, pool_w, pool_scale.reshape(1, POOL_CH), ln_g.reshape(1, GMLP_CH),
      ln_b.reshape(1, GMLP_CH), ws, bs_t)


def _trunk(x, n_seq, seq_len, cond_map, mods0, mods1, h0, p):
    z = _inproj(x, p["l0_g_mix_pre"], mods0, cond_map, p["l0_w_in"])
    yf, yb, hfin = _s5_scan(z, n_seq, seq_len, h0, p["a_vec"], p["b_blk"], p["c_blk"])
    ya = _s5_glu(yf, yb, z, p["l0_s5_d"], p["l0_s5_w_glu"], p["l0_s5_b_glu"])
    yfn = _fnet(z, n_seq, seq_len, p["fnet_w_blk"], p["fnet_b"])
    x = _outproj([ya, yfn], [p["l0_w_out_a"], p["l0_w_out_b"]], x, p["l0_g_mix_post"], mods0, cond_map)
    x = _ffn(x, p["l0_g_ff_pre"], p["l0_g_ff_post"], mods0, cond_map, p["l0_w_ff1"], p["l0_w_ff2"])
    z = _inproj(x, p["l1_g_mix_pre"], mods1, cond_map, p["l1_w_in"])
    y = _odd_mixer(z, n_seq, seq_len, p["pool_w"], p["l1_pool_scale"], p["l1_gmlp_ln_g"],
                   p["l1_gmlp_ln_b"], p["gmlp_ws"], p["gmlp_bs_t"], tt=min(seq_len, 512))
    x = _outproj([y], [p["l1_w_out"]], x, p["l1_g_mix_post"], mods1, cond_map)
    x = _ffn(x, p["l1_g_ff_pre"], p["l1_g_ff_post"], mods1, cond_map, p["l1_w_ff1"], p["l1_w_ff2"])
    return x, hfin


def kernel(x_prompt, x_sample, state_l0_s5_re, state_l0_s5_im, c, c_ctx, l0_w_mod, l0_b_mod, l0_g_mix_pre, l0_g_mix_post, l0_g_ff_pre, l0_g_ff_post, l0_w_ff1, l0_w_ff2, l0_w_in, l0_w_out, l0_s5_lambda_re, l0_s5_lambda_im, l0_s5_log_dt, l0_s5_b_re, l0_s5_b_im, l0_s5_c_re, l0_s5_c_im, l0_s5_d, l0_s5_w_glu, l0_s5_b_glu, l0_fnet_w, l0_fnet_b, l1_w_mod, l1_b_mod, l1_g_mix_pre, l1_g_mix_post, l1_g_ff_pre, l1_g_ff_post, l1_w_ff1, l1_w_ff2, l1_w_in, l1_w_out, l1_pool_w, l1_pool_scale, l1_gmlp_ln_g, l1_gmlp_ln_b, l1_gmlp_ws, l1_gmlp_bs):
    batch, seq, _ = x_prompt.shape
    dec_batch, dec_seq, _ = x_sample.shape
    assert dec_seq == GRID_W * GRID_W and seq % S5_TT == 0

    conds = jnp.zeros((8, D), F32).at[0].set(c_ctx).at[1:1 + dec_batch].set(c)
    mods0 = _modulation(conds, l0_w_mod, l0_b_mod)
    mods1 = _modulation(conds, l1_w_mod, l1_b_mod)

    ab_re, ab_im, bb_re, bb_im = _s5_discretize(l0_s5_lambda_re, l0_s5_lambda_im, l0_s5_log_dt,
                                                l0_s5_b_re, l0_s5_b_im)
    b_blk, c_blk, a_vec = _s5_block_params(ab_re, ab_im, bb_re, bb_im, l0_s5_c_re, l0_s5_c_im)

    eye4 = jnp.eye(FNET_CH // FNET_GW, dtype=F32)
    fnet_w_blk = jnp.einsum("gcd,gk->gckd", l0_fnet_w, eye4).reshape(FNET_CH, FNET_CH).astype(BF16)

    p = dict(
        l0_g_mix_pre=l0_g_mix_pre, l0_g_mix_post=l0_g_mix_post, l0_g_ff_pre=l0_g_ff_pre,
        l0_g_ff_post=l0_g_ff_post, l0_w_ff1=l0_w_ff1.astype(BF16), l0_w_ff2=l0_w_ff2.astype(BF16),
        l0_w_in=l0_w_in.astype(BF16), l0_w_out_a=l0_w_out[:S5_CH].astype(BF16),
        l0_w_out_b=l0_w_out[S5_CH:].astype(BF16), l0_s5_d=l0_s5_d,
        l0_s5_w_glu=l0_s5_w_glu.astype(BF16), l0_s5_b_glu=l0_s5_b_glu,
        a_vec=a_vec, b_blk=b_blk, c_blk=c_blk, fnet_w_blk=fnet_w_blk,
        fnet_b=l0_fnet_b.reshape(1, FNET_CH),
        l1_g_mix_pre=l1_g_mix_pre, l1_g_mix_post=l1_g_mix_post, l1_g_ff_pre=l1_g_ff_pre,
        l1_g_ff_post=l1_g_ff_post, l1_w_ff1=l1_w_ff1.astype(BF16), l1_w_ff2=l1_w_ff2.astype(BF16),
        l1_w_in=l1_w_in.astype(BF16), l1_w_out=l1_w_out.astype(BF16),
        pool_w=l1_pool_w.astype(BF16), l1_pool_scale=l1_pool_scale, l1_gmlp_ln_g=l1_gmlp_ln_g,
        l1_gmlp_ln_b=l1_gmlp_ln_b, gmlp_ws=l1_gmlp_ws.astype(BF16), gmlp_bs_t=l1_gmlp_bs.T,
    )

    zero_state = jnp.zeros((batch, 12, SUBLANES, LANES), F32)
    y_prompt, hfin = _trunk(x_prompt.reshape(batch * seq, D), batch, seq, lambda r: 0,
                            mods0, mods1, zero_state, p)
    new_re, new_im = _unpack_state(hfin)

    xs = _embed(x_sample.reshape(dec_batch * dec_seq, D))
    h0 = _pack_state(state_l0_s5_re, state_l0_s5_im)
    y_sample, _ = _trunk(xs, dec_batch, dec_seq, lambda r: 1 + r // dec_seq, mods0, mods1, h0, p)

    return (y_prompt.reshape(batch, seq, D), y_sample.reshape(dec_batch, dec_seq, D), new_re, new_im)
```

```python
import functools
import math

import jax
import jax.numpy as jnp
from jax import lax
from jax.experimental import pallas as pl
from jax.experimental.pallas import tpu as pltpu

F32 = jnp.float32
BF16 = jnp.bfloat16
I32 = jnp.int32

D = 1024
D_FF = 4 * D
N_MOD = 6
EPS = 1e-6
GRID_W = 64
POS_BASE = 10000.0

S5_CH = 768
S5_GROUPS = 48
S5_GROUP = 16
S5_STATE = 64
S5_CLUSTERS = 3
CL_GROUPS = 16
CL_CH = 256
CL_STATES = 1024
FNET_CH = 256
FNET_GW = 64

POOL_WINDOWS = (2, 4, 8, 16)
POOL_CH = 512
POOL_GW = 128
POOL_HALO = 8
GMLP_CH = 512
GMLP_HEADS = 4
GMLP_HD = 128
GMLP_CHUNK = 128

LANES = 128
SUBLANES = 8
VMEM_LIMIT = 56 * 1024 * 1024

S5_TT = 128
TM = 512
FF_CHUNKS = 4


def _cparams(sem):
    return pltpu.CompilerParams(dimension_semantics=sem, vmem_limit_bytes=VMEM_LIMIT)


def _resident(shape):
    nd = len(shape)
    return pl.BlockSpec(shape, lambda *_: (0,) * nd, pipeline_mode=pl.Buffered(1))


def _rows(width, tm=TM):
    return pl.BlockSpec((tm, width), lambda i: (i, 0))


def _mod_spec(cond_map, tm=TM):
    return pl.BlockSpec((1, 1, N_MOD * D), lambda i: (cond_map(i * tm), 0, 0))


def _rms(x, g):
    return x * lax.rsqrt(jnp.mean(x * x, axis=-1, keepdims=True) + EPS) * g


def _mod(m, idx):
    return m[:, idx * D:(idx + 1) * D]


def _ffn_sublayer(x, m, gpre, gpost, w1_ref, w2_ref, a_ref):
    h = (_rms(x, gpre) * (1.0 + _mod(m, 4)) + _mod(m, 3)).astype(BF16)
    cw = D_FF // FF_CHUNKS
    for f in range(FF_CHUNKS):
        cols = slice(f * cw, (f + 1) * cw)
        a = jnp.dot(h, w1_ref[:, cols], preferred_element_type=F32)
        a_ref[:, cols] = jnp.square(jnp.maximum(a, 0.0)).astype(BF16)
    y = jnp.dot(a_ref[...], w2_ref[...], preferred_element_type=F32)
    return x + _mod(m, 5) * _rms(y, gpost)


def _mod_kernel(c_ref, w_ref, b_ref, o_ref):
    c = c_ref[...]
    s = c * jax.nn.sigmoid(c)
    o_ref[...] = jnp.dot(s.astype(BF16), w_ref[...].astype(BF16),
                         preferred_element_type=F32) + b_ref[...]


def _modulation(conds, w_mod, b_mod):
    tn = 1024
    out = pl.pallas_call(
        _mod_kernel,
        out_shape=jax.ShapeDtypeStruct((8, N_MOD * D), F32),
        grid=(N_MOD * D // tn,),
        in_specs=[pl.BlockSpec((8, D), lambda j: (0, 0)),
                  pl.BlockSpec((D, tn), lambda j: (0, j)),
                  pl.BlockSpec((1, tn), lambda j: (0, j))],
        out_specs=pl.BlockSpec((8, tn), lambda j: (0, j)),
        compiler_params=_cparams(("parallel",)),
        name="modulation",
    )(conds, w_mod, b_mod.reshape(1, -1))
    return out.reshape(8, 1, N_MOD * D)


def _pos_table_kernel(o_ref):
    quarter = D // 4
    k = lax.broadcasted_iota(I32, (GRID_W, quarter), 1).astype(F32)
    p = lax.broadcasted_iota(I32, (GRID_W, quarter), 0).astype(F32)
    ang = p * jnp.exp(k * (-math.log(POS_BASE) / quarter))
    o_ref[:, 0:quarter] = jnp.sin(ang)
    o_ref[:, quarter:] = jnp.cos(ang)


def _pos_table():
    return pl.pallas_call(
        _pos_table_kernel,
        out_shape=jax.ShapeDtypeStruct((GRID_W, D // 2), F32),
        name="pos_table",
    )()


def _add_pos(x_ref, tab_ref, xs_ref, tm):
    i = pl.program_id(0)
    half = D // 2
    row0 = (i * (tm // GRID_W)) % GRID_W
    for r in range(tm // GRID_W):
        sl = slice(r * GRID_W, (r + 1) * GRID_W)
        xs_ref[sl, 0:half] = x_ref[sl, 0:half] + tab_ref[pl.ds(row0 + r, 1), :]
        xs_ref[sl, half:] = x_ref[sl, half:] + tab_ref[...]
    return xs_ref[...]


def _inproj_kernel(*refs, with_pos, tm):
    if with_pos:
        x_ref, tab_ref, g_ref, m_ref, w_ref, o_ref, xs_ref = refs
        x = _add_pos(x_ref, tab_ref, xs_ref, tm)
    else:
        x_ref, g_ref, m_ref, w_ref, o_ref = refs
        x = x_ref[...]
    m = m_ref[0]
    h = _rms(x, g_ref[...]) * (1.0 + _mod(m, 1)) + _mod(m, 0)
    o_ref[...] = jnp.dot(h.astype(BF16), w_ref[...], preferred_element_type=F32)


def _inproj(x, tab, g, mods, cond_map, w_bf16, tm=TM):
    n = x.shape[0]
    nout = w_bf16.shape[1]
    with_pos = tab is not None
    args = [x] + ([tab] if with_pos else []) + [g.reshape(1, D), mods, w_bf16]
    in_specs = [_rows(D, tm)] + ([_resident(tab.shape)] if with_pos else [])
    in_specs += [_resident((1, D)), _mod_spec(cond_map, tm), _resident(w_bf16.shape)]
    return pl.pallas_call(
        functools.partial(_inproj_kernel, with_pos=with_pos, tm=tm),
        out_shape=jax.ShapeDtypeStruct((n, nout), F32),
        grid=(n // tm,),
        in_specs=in_specs,
        out_specs=_rows(nout, tm),
        scratch_shapes=[pltpu.VMEM((tm, D), F32)] if with_pos else [],
        compiler_params=_cparams(("parallel",)),
        name="in_proj0",
    )(*args)


def _s5_params_kernel(lre_ref, lim_ref, ldt_ref, bre_ref, bim_ref, cre_ref, cim_ref,
                      are_ref, aim_ref, bblk_ref, cblk_ref):
    lam_re = lre_ref[...]
    lam_im = lim_ref[...]
    dt = jnp.exp(ldt_ref[...])
    mag = jnp.exp(lam_re * dt)
    ab_re = mag * jnp.cos(lam_im * dt)
    ab_im = mag * jnp.sin(lam_im * dt)
    num_re = ab_re - 1.0
    num_im = ab_im
    den = lam_re * lam_re + lam_im * lam_im
    coef_re = (num_re * lam_re + num_im * lam_im) / den
    coef_im = (num_im * lam_re - num_re * lam_im) / den
    b_re = bre_ref[...]
    b_im = bim_ref[...]
    are_ref[...] = ab_re
    aim_ref[...] = ab_im
    bb = (coef_re * b_re - coef_im * b_im, coef_re * b_im + coef_im * b_re)

    bblk_ref[...] = jnp.zeros_like(bblk_ref)
    cblk_ref[...] = jnp.zeros_like(cblk_ref)

    pr = 2 * S5_GROUP
    r = lax.broadcasted_iota(I32, (pr, LANES), 0)
    l = lax.broadcasted_iota(I32, (pr, LANES), 1)
    pair_mask = (r < S5_GROUP) == (l < S5_STATE)
    for d in range(2):
        for c in range(S5_CLUSTERS):
            for ri in range(2):
                for mth in range(CL_GROUPS // 2):
                    r0 = ((d * S5_GROUPS + c * CL_GROUPS + 2 * mth) * S5_GROUP)
                    tile = jnp.where(pair_mask, bb[ri][r0:r0 + pr, :], 0.0)
                    col = (ri * SUBLANES + mth) * LANES
                    bblk_ref[d, c, mth * pr:(mth + 1) * pr, col:col + LANES] = tile.astype(BF16)

    gr = SUBLANES * S5_STATE
    r = lax.broadcasted_iota(I32, (gr, LANES), 0)
    l = lax.broadcasted_iota(I32, (gr, LANES), 1)
    oct_mask = (r >> 6) == (l >> 4)
    for d in range(2):
        for c in range(S5_CLUSTERS):
            for ri, c_ref in enumerate((cre_ref, cim_ref)):
                for n in range(CL_GROUPS // SUBLANES):
                    s0 = (d * S5_GROUPS + c * CL_GROUPS + SUBLANES * n) * S5_STATE
                    v = c_ref[s0:s0 + gr, :]
                    tile = jnp.where(oct_mask, v if ri == 0 else -v, 0.0)
                    row = ri * CL_STATES + n * gr
                    cblk_ref[d, c, row:row + gr, n * LANES:(n + 1) * LANES] = tile.astype(BF16)


def _s5_params(lam_re, lam_im, log_dt, b_re, b_im, c_re, c_im):
    rows = 2 * S5_GROUPS * S5_GROUP
    shp = (2, S5_GROUPS, S5_GROUP, S5_STATE)
    dup = lambda a: jnp.tile(a.reshape(rows, S5_STATE), (1, 2))
    rep = lambda a: dup(jnp.broadcast_to(a[:, :, None, :], shp))
    ldt = dup(jnp.broadcast_to(log_dt[:, :, None, None], shp))
    tr_b = lambda b: dup(jnp.swapaxes(b, 2, 3))
    crow = 2 * S5_GROUPS * S5_STATE
    tr_c = lambda c: jnp.tile(jnp.swapaxes(c, 2, 3).reshape(crow, S5_GROUP), (1, SUBLANES))
    ab_re, ab_im, b_blk, c_blk = pl.pallas_call(
        _s5_params_kernel,
        out_shape=[jax.ShapeDtypeStruct((rows, LANES), F32),
                   jax.ShapeDtypeStruct((rows, LANES), F32),
                   jax.ShapeDtypeStruct((2, S5_CLUSTERS, CL_CH, 2 * CL_STATES), BF16),
                   jax.ShapeDtypeStruct((2, S5_CLUSTERS, 2 * CL_STATES, CL_CH), BF16)],
        compiler_params=pltpu.CompilerParams(vmem_limit_bytes=VMEM_LIMIT),
        name="s5_params",
    )(rep(lam_re), rep(lam_im), ldt, tr_b(b_re), tr_b(b_im), tr_c(c_re), tr_c(c_im))
    pick = lambda a: a.reshape(2, S5_GROUPS, S5_GROUP, LANES)[:, :, 0, :S5_STATE]
    a_vec = jnp.stack([pick(ab_re), pick(ab_im)], axis=0)
    a_vec = a_vec.reshape(2, 2, S5_CLUSTERS, SUBLANES, LANES)
    a_vec = jnp.transpose(a_vec, (1, 2, 0, 3, 4)).reshape(12, SUBLANES, LANES)
    return b_blk, c_blk, a_vec


def _pack_state(s_re, s_im):
    b = s_re.shape[0]
    st = jnp.stack([s_re, s_im], axis=2)
    st = st.reshape(b, 2, 2, S5_CLUSTERS, SUBLANES, LANES)
    return jnp.transpose(st, (0, 1, 3, 2, 4, 5)).reshape(b, 12, SUBLANES, LANES)


def _unpack_state(st):
    b = st.shape[0]
    st = st.reshape(b, 2, S5_CLUSTERS, 2, SUBLANES, LANES)
    st = jnp.transpose(st, (0, 3, 1, 2, 4, 5)).reshape(b, 2, 2, S5_GROUPS, S5_STATE)
    return st[:, 0], st[:, 1]


def _s5_kernel(uf_ref, ub_ref, h0_ref, a_ref, bblk_ref, cblk_ref, yf_ref, yb_ref, hf_ref,
               s_ref, h_ref, st_ref, *, tt):
    j = pl.program_id(1)
    nj = pl.num_programs(1)
    im0 = tt * SUBLANES

    @pl.when(j == 0)
    def _():
        st_ref[...] = h0_ref[0]

    for d, u_ref in enumerate((uf_ref, ub_ref)):
        for c in range(S5_CLUSTERS):
            k = d * S5_CLUSTERS + c
            u = u_ref[:, c * CL_CH:(c + 1) * CL_CH].astype(BF16)
            res = jnp.dot(u, bblk_ref[d, c], preferred_element_type=F32)
            for q in range(SUBLANES):
                s_ref[k, pl.ds(q, tt, stride=SUBLANES), :] = res[:, q * LANES:(q + 1) * LANES]
                s_ref[k, pl.ds(im0 + q, tt, stride=SUBLANES), :] = (
                    res[:, CL_STATES + q * LANES:CL_STATES + (q + 1) * LANES])

    a = [a_ref[i] for i in range(12)]

    def body(t, carry):
        new = []
        for k in range(2 * S5_CLUSTERS):
            row = pl.multiple_of((t if k < S5_CLUSTERS else tt - 1 - t) * SUBLANES, SUBLANES)
            hr, hi = carry[2 * k], carry[2 * k + 1]
            ar, ai = a[2 * k], a[2 * k + 1]
            br = s_ref[k, pl.ds(row, SUBLANES), :]
            bi = s_ref[k, pl.ds(im0 + row, SUBLANES), :]
            nr = ar * hr - ai * hi + br
            ni = ar * hi + ai * hr + bi
            h_ref[k, pl.ds(row, SUBLANES), :] = nr
            h_ref[k, pl.ds(im0 + row, SUBLANES), :] = ni
            new += [nr, ni]
        return tuple(new)

    fin = lax.fori_loop(0, tt, body, tuple(st_ref[i] for i in range(12)), unroll=4)
    for i in range(12):
        st_ref[i] = fin[i]

    for d, y_ref in enumerate((yf_ref, yb_ref)):
        for c in range(S5_CLUSTERS):
            k = d * S5_CLUSTERS + c
            hcat = jnp.concatenate(
                [h_ref[k, pl.ds(base + q, tt, stride=SUBLANES), :].astype(BF16)
                 for base in (0, im0) for q in range(SUBLANES)], axis=-1)
            y_ref[:, c * CL_CH:(c + 1) * CL_CH] = jnp.dot(
                hcat, cblk_ref[d, c], preferred_element_type=F32)

    @pl.when(j == nj - 1)
    def _():
        hf_ref[0] = st_ref[...]


def _s5_scan(z, n_seq, seq_len, h0, a_vec, b_blk, c_blk):
    tt = S5_TT
    nj = seq_len // tt
    n = n_seq * seq_len
    st_spec = pl.BlockSpec((1, 12, SUBLANES, LANES), lambda s, j: (s, 0, 0, 0))
    return pl.pallas_call(
        functools.partial(_s5_kernel, tt=tt),
        out_shape=[jax.ShapeDtypeStruct((n, S5_CH), F32),
                   jax.ShapeDtypeStruct((n, S5_CH), F32),
                   jax.ShapeDtypeStruct((n_seq, 12, SUBLANES, LANES), F32)],
        grid=(n_seq, nj),
        in_specs=[pl.BlockSpec((tt, S5_CH), lambda s, j: (s * nj + j, 0)),
                  pl.BlockSpec((tt, S5_CH), lambda s, j: (s * nj + nj - 1 - j, 0)),
                  st_spec,
                  _resident((12, SUBLANES, LANES)),
                  _resident(b_blk.shape),
                  _resident(c_blk.shape)],
        out_specs=[pl.BlockSpec((tt, S5_CH), lambda s, j: (s * nj + j, 0)),
                   pl.BlockSpec((tt, S5_CH), lambda s, j: (s * nj + nj - 1 - j, 0)),
                   st_spec],
        scratch_shapes=[pltpu.VMEM((2 * S5_CLUSTERS, 2 * SUBLANES * tt, LANES), F32),
                        pltpu.VMEM((2 * S5_CLUSTERS, 2 * SUBLANES * tt, LANES), F32),
                        pltpu.VMEM((12, SUBLANES, LANES), F32)],
        compiler_params=_cparams(("arbitrary", "arbitrary")),
        name="s5_scan",
    )(z, z, h0, a_vec, b_blk, c_blk)


def _fnet_kernel(u_ref, wf_ref, bf_ref, o_ref, cs_ref, cb_ref, sb_ref, vc_ref, vs_ref,
                 *, seq_len, tm, kc):
    s = pl.program_id(0)
    i = pl.program_id(1)
    two_pi = 2.0 * math.pi

    @pl.when(jnp.logical_and(s == 0, i == 0))
    def _():
        r = lax.broadcasted_iota(I32, (FNET_CH, FNET_CH), 0)
        k = lax.broadcasted_iota(I32, (FNET_CH, FNET_CH), 1)
        same = (r >> 6) == (k >> 6)
        ang = (((r & 63) * (k & 63)) & 63).astype(F32) * (two_pi / FNET_GW)
        cs_ref[:, 0:FNET_CH] = jnp.where(same, jnp.cos(ang), 0.0).astype(BF16)
        cs_ref[:, FNET_CH:] = jnp.where(same, jnp.sin(ang), 0.0).astype(BF16)

        def tab(rb, carry):
            r0 = pl.multiple_of(rb * SUBLANES, SUBLANES)
            j0 = lax.broadcasted_iota(I32, (SUBLANES, seq_len), 0) + r0
            kk = lax.broadcasted_iota(I32, (SUBLANES, seq_len), 1)
            a = ((j0 * kk) & (seq_len - 1)).astype(F32) * (two_pi / seq_len)
            cb_ref[pl.ds(r0, SUBLANES), :] = jnp.cos(a)
            sb_ref[pl.ds(r0, SUBLANES), :] = jnp.sin(a)
            return carry

        lax.fori_loop(0, tm // SUBLANES, tab, 0)

    @pl.when(i == 0)
    def _():
        rc = min(seq_len, 512)
        for r in range(seq_len // rc):
            sl = slice(r * rc, (r + 1) * rc)
            v = jnp.dot(u_ref[sl, :].astype(BF16), cs_ref[...], preferred_element_type=F32)
            vc_ref[sl, :] = v[:, 0:FNET_CH].astype(BF16)
            vs_ref[sl, :] = v[:, FNET_CH:].astype(BF16)

    kk = lax.broadcasted_iota(I32, (1, seq_len), 1)
    ang = (((i * tm) * kk) & (seq_len - 1)).astype(F32) * (two_pi / seq_len)
    ca = jnp.cos(ang)
    sa = jnp.sin(ang)
    acc = jnp.zeros((tm, FNET_CH), F32)
    for kci in range(seq_len // kc):
        sl = slice(kci * kc, (kci + 1) * kc)
        cb = cb_ref[:, sl]
        sb = sb_ref[:, sl]
        cmat = (ca[:, sl] * cb - sa[:, sl] * sb).astype(BF16)
        smat = (sa[:, sl] * cb + ca[:, sl] * sb).astype(BF16)
        acc = acc + jnp.dot(cmat, vc_ref[sl, :], preferred_element_type=F32)
        acc = acc - jnp.dot(smat, vs_ref[sl, :], preferred_element_type=F32)
    zf = acc * (1.0 / math.sqrt(seq_len * FNET_GW))
    o_ref[...] = jnp.dot(zf.astype(BF16), wf_ref[...], preferred_element_type=F32) + bf_ref[...]


def _fnet(z, n_seq, seq_len, w_blk, b_vec):
    tm = 256
    kc = min(seq_len, 1024)
    ni = seq_len // tm
    n = n_seq * seq_len
    col = (D - FNET_CH) // FNET_CH
    return pl.pallas_call(
        functools.partial(_fnet_kernel, seq_len=seq_len, tm=tm, kc=kc),
        out_shape=jax.ShapeDtypeStruct((n, FNET_CH), F32),
        grid=(n_seq, ni),
        in_specs=[pl.BlockSpec((seq_len, FNET_CH), lambda s, i: (s, col)),
                  pl.BlockSpec((FNET_CH, FNET_CH), lambda s, i: (0, 0)),
                  pl.BlockSpec((1, FNET_CH), lambda s, i: (0, 0))],
        out_specs=pl.BlockSpec((tm, FNET_CH), lambda s, i: (s * ni + i, 0)),
        scratch_shapes=[pltpu.VMEM((FNET_CH, 2 * FNET_CH), BF16),
                        pltpu.VMEM((tm, seq_len), F32),
                        pltpu.VMEM((tm, seq_len), F32),
                        pltpu.VMEM((seq_len, FNET_CH), BF16),
                        pltpu.VMEM((seq_len, FNET_CH), BF16)],
        compiler_params=_cparams(("arbitrary", "arbitrary")),
        name="fnet",
    )(z, w_blk, b_vec)


def _post0_kernel(*refs, with_pos, tm):
    if with_pos:
        (yf_ref, yb_ref, u_ref, fn_ref, x_ref, tab_ref, d_ref, wglu_ref, bglu_ref, woa_ref,
         wob_ref, gpost_ref, gfpre_ref, gfpost_ref, m_ref, w1_ref, w2_ref, o_ref, a_ref, xs_ref) = refs
        x = _add_pos(x_ref, tab_ref, xs_ref, tm)
    else:
        (yf_ref, yb_ref, u_ref, fn_ref, x_ref, d_ref, wglu_ref, bglu_ref, woa_ref,
         wob_ref, gpost_ref, gfpre_ref, gfpost_ref, m_ref, w1_ref, w2_ref, o_ref, a_ref) = refs
        x = x_ref[...]
    m = m_ref[0]
    y = yf_ref[...] + yb_ref[...] + d_ref[...] * u_ref[...]
    g = jax.nn.gelu(y)
    gate = jnp.dot(g.astype(BF16), wglu_ref[...], preferred_element_type=F32) + bglu_ref[...]
    ya = g * jax.nn.sigmoid(gate)
    o = jnp.dot(ya.astype(BF16), woa_ref[...], preferred_element_type=F32)
    o = o + jnp.dot(fn_ref[...].astype(BF16), wob_ref[...], preferred_element_type=F32)
    x = x + _mod(m, 2) * _rms(o, gpost_ref[...])
    o_ref[...] = _ffn_sublayer(x, m, gfpre_ref[...], gfpost_ref[...], w1_ref, w2_ref, a_ref)


def _post0(yf, yb, z, yfn, x, tab, mods, cond_map, p, tm=TM):
    n = x.shape[0]
    with_pos = tab is not None
    vec = lambda a, w: a.reshape(1, w)
    args = [yf, yb, z, yfn, x] + ([tab] if with_pos else [])
    in_specs = [_rows(S5_CH, tm)] * 3 + [_rows(FNET_CH, tm), _rows(D, tm)]
    in_specs += [_resident(tab.shape)] if with_pos else []
    consts = [vec(p["l0_s5_d"], S5_CH), p["l0_s5_w_glu"], vec(p["l0_s5_b_glu"], S5_CH),
              p["l0_w_out_a"], p["l0_w_out_b"], vec(p["l0_g_mix_post"], D),
              vec(p["l0_g_ff_pre"], D), vec(p["l0_g_ff_post"], D)]
    args += consts + [mods, p["l0_w_ff1"], p["l0_w_ff2"]]
    in_specs += [_resident(a.shape) for a in consts] + [_mod_spec(cond_map, tm),
                                                        _resident((D, D_FF)), _resident((D_FF, D))]
    scratch = [pltpu.VMEM((tm, D_FF), BF16)] + ([pltpu.VMEM((tm, D), F32)] if with_pos else [])
    return pl.pallas_call(
        functools.partial(_post0_kernel, with_pos=with_pos, tm=tm),
        out_shape=jax.ShapeDtypeStruct((n, D), F32),
        grid=(n // tm,),
        in_specs=in_specs,
        out_specs=_rows(D, tm),
        scratch_shapes=scratch,
        compiler_params=_cparams(("parallel",)),
        name="post0",
    )(*args)


def _layer1_kernel(x_ref, xp_ref, xn_ref, gpre_ref, m_ref, win_ref, pw_ref, ps_ref, lng_ref,
                   lnb_ref, ws_ref, bs_ref, wout_ref, gpost_ref, gfpre_ref, gfpost_ref,
                   w1_ref, w2_ref, o_ref, xe_ref, y_ref, a_ref, *, tm, seq_len):
    i = pl.program_id(0)
    hl = POOL_HALO
    m = m_ref[0]
    x = x_ref[...]
    scale = 1.0 + _mod(m, 1)
    shift = _mod(m, 0)
    h = (_rms(x, gpre_ref[...]) * scale + shift).astype(BF16)

    r0 = i * tm
    prev_ok = (r0 & (seq_len - 1)) != 0
    next_ok = ((r0 + tm) & (seq_len - 1)) != 0
    xh = jnp.concatenate([xp_ref[...], xn_ref[...]], axis=0)
    hh = (_rms(xh, gpre_ref[...]) * scale + shift).astype(BF16)
    zh = jnp.dot(hh, win_ref[:, 0:POOL_CH], preferred_element_type=F32)
    xe_ref[0:hl, :] = jnp.where(prev_ok, zh[0:hl], 0.0)
    xe_ref[hl + tm:hl + tm + hl, :] = jnp.where(next_ok, zh[hl:], 0.0)
    xe_ref[hl:hl + tm, :] = jnp.dot(h, win_ref[:, 0:POOL_CH], preferred_element_type=F32)

    t = (r0 + lax.broadcasted_iota(I32, (tm, 1), 0)) & (seq_len - 1)
    for gi, win in enumerate(POOL_WINDOWS):
        half = win // 2
        cols = slice(gi * POOL_GW, (gi + 1) * POOL_GW)
        acc = None
        for o in range(-half, half):
            v = xe_ref[pl.ds(hl + o, tm), cols]
            if o != 0:
                v = jnp.where((t >= -o) if o < 0 else (t < seq_len - o), v, 0.0)
            acc = v if acc is None else acc + v
        cnt = (jnp.minimum(t + half, seq_len) - jnp.maximum(t - half, 0)).astype(F32)
        pooled = acc / cnt - xe_ref[hl:hl + tm, cols]
        mixed = jnp.dot(pooled.astype(BF16), pw_ref[gi], preferred_element_type=F32)
        y_ref[:, cols] = (mixed * ps_ref[:, cols]).astype(BF16)

    zu = jax.nn.gelu(jnp.dot(h, win_ref[:, POOL_CH:POOL_CH + GMLP_CH], preferred_element_type=F32))
    zv = jax.nn.gelu(jnp.dot(h, win_ref[:, POOL_CH + GMLP_CH:], preferred_element_type=F32))
    for hd in range(GMLP_HEADS):
        hc = slice(hd * GMLP_HD, (hd + 1) * GMLP_HD)
        v = zv[:, hc]
        vc = v - jnp.mean(v, axis=-1, keepdims=True)
        var = jnp.mean(vc * vc, axis=-1, keepdims=True)
        vn = (vc * lax.rsqrt(var + EPS) * lng_ref[:, hc] + lnb_ref[:, hc]).astype(BF16)
        for ch in range(tm // GMLP_CHUNK):
            rows = slice(ch * GMLP_CHUNK, (ch + 1) * GMLP_CHUNK)
            sv = jnp.dot(ws_ref[hd], vn[rows, :], preferred_element_type=F32) + bs_ref[:, hd:hd + 1]
            y_ref[rows, POOL_CH + hd * GMLP_HD:POOL_CH + (hd + 1) * GMLP_HD] = (
                zu[rows, hc] * sv).astype(BF16)

    o = jnp.dot(y_ref[...], wout_ref[...], preferred_element_type=F32)
    x = x + _mod(m, 2) * _rms(o, gpost_ref[...])
    o_ref[...] = _ffn_sublayer(x, m, gfpre_ref[...], gfpost_ref[...], w1_ref, w2_ref, a_ref)


def _layer1(x, seq_len, mods, cond_map, p, tm=TM):
    n = x.shape[0]
    hb = tm // POOL_HALO
    nhb = n // POOL_HALO
    vec = lambda a, w: a.reshape(1, w)
    consts_a = [vec(p["l1_g_mix_pre"], D)]
    consts_b = [p["l1_w_in"], p["pool_w"], vec(p["l1_pool_scale"], POOL_CH),
                vec(p["l1_gmlp_ln_g"], GMLP_CH), vec(p["l1_gmlp_ln_b"], GMLP_CH), p["gmlp_ws"],
                p["gmlp_bs_t"], p["l1_w_out"], vec(p["l1_g_mix_post"], D),
                vec(p["l1_g_ff_pre"], D), vec(p["l1_g_ff_post"], D), p["l1_w_ff1"], p["l1_w_ff2"]]
    in_specs = [_rows(D, tm),
                pl.BlockSpec((POOL_HALO, D), lambda i: (jnp.maximum(i * hb - 1, 0), 0)),
                pl.BlockSpec((POOL_HALO, D), lambda i: (jnp.minimum((i + 1) * hb, nhb - 1), 0))]
    in_specs += [_resident(a.shape) for a in consts_a] + [_mod_spec(cond_map, tm)]
    in_specs += [_resident(a.shape) for a in consts_b]
    return pl.pallas_call(
        functools.partial(_layer1_kernel, tm=tm, seq_len=seq_len),
        out_shape=jax.ShapeDtypeStruct((n, D), F32),
        grid=(n // tm,),
        in_specs=in_specs,
        out_specs=_rows(D, tm),
        scratch_shapes=[pltpu.VMEM((tm + 2 * POOL_HALO, POOL_CH), F32),
                        pltpu.VMEM((tm, D), BF16),
                        pltpu.VMEM((tm, D_FF), BF16)],
        compiler_params=_cparams(("parallel",)),
        name="layer1",
    )(x, x, x, *consts_a, mods, *consts_b)


def _trunk(x, tab, n_seq, seq_len, cond_map, mods0, mods1, h0, p):
    z = _inproj(x, tab, p["l0_g_mix_pre"], mods0, cond_map, p["l0_w_in"])
    yf, yb, hfin = _s5_scan(z, n_seq, seq_len, h0, p["a_vec"], p["b_blk"], p["c_blk"])
    yfn = _fnet(z, n_seq, seq_len, p["fnet_w_blk"], p["fnet_b"])
    x = _post0(yf, yb, z, yfn, x, tab, mods0, cond_map, p)
    x = _layer1(x, seq_len, mods1, cond_map, p)
    return x, hfin


def kernel(x_prompt, x_sample, state_l0_s5_re, state_l0_s5_im, c, c_ctx, l0_w_mod, l0_b_mod, l0_g_mix_pre, l0_g_mix_post, l0_g_ff_pre, l0_g_ff_post, l0_w_ff1, l0_w_ff2, l0_w_in, l0_w_out, l0_s5_lambda_re, l0_s5_lambda_im, l0_s5_log_dt, l0_s5_b_re, l0_s5_b_im, l0_s5_c_re, l0_s5_c_im, l0_s5_d, l0_s5_w_glu, l0_s5_b_glu, l0_fnet_w, l0_fnet_b, l1_w_mod, l1_b_mod, l1_g_mix_pre, l1_g_mix_post, l1_g_ff_pre, l1_g_ff_post, l1_w_ff1, l1_w_ff2, l1_w_in, l1_w_out, l1_pool_w, l1_pool_scale, l1_gmlp_ln_g, l1_gmlp_ln_b, l1_gmlp_ws, l1_gmlp_bs):
    batch, seq, _ = x_prompt.shape
    dec_batch, dec_seq, _ = x_sample.shape
    assert dec_seq == GRID_W * GRID_W and seq % S5_TT == 0 and TM % seq == 0

    conds = jnp.zeros((8, D), F32).at[0].set(c_ctx).at[1:1 + dec_batch].set(c)
    mods0 = _modulation(conds, l0_w_mod, l0_b_mod)
    mods1 = _modulation(conds, l1_w_mod, l1_b_mod)

    b_blk, c_blk, a_vec = _s5_params(l0_s5_lambda_re, l0_s5_lambda_im, l0_s5_log_dt,
                                     l0_s5_b_re, l0_s5_b_im, l0_s5_c_re, l0_s5_c_im)

    eye4 = jnp.eye(FNET_CH // FNET_GW, dtype=F32)
    fnet_w_blk = jnp.einsum("gcd,gk->gckd", l0_fnet_w, eye4).reshape(FNET_CH, FNET_CH).astype(BF16)

    p = dict(
        l0_g_mix_pre=l0_g_mix_pre, l0_g_mix_post=l0_g_mix_post, l0_g_ff_pre=l0_g_ff_pre,
        l0_g_ff_post=l0_g_ff_post, l0_w_ff1=l0_w_ff1.astype(BF16), l0_w_ff2=l0_w_ff2.astype(BF16),
        l0_w_in=l0_w_in.astype(BF16), l0_w_out_a=l0_w_out[:S5_CH].astype(BF16),
        l0_w_out_b=l0_w_out[S5_CH:].astype(BF16), l0_s5_d=l0_s5_d,
        l0_s5_w_glu=l0_s5_w_glu.astype(BF16), l0_s5_b_glu=l0_s5_b_glu,
        a_vec=a_vec, b_blk=b_blk, c_blk=c_blk, fnet_w_blk=fnet_w_blk,
        fnet_b=l0_fnet_b.reshape(1, FNET_CH),
        l1_g_mix_pre=l1_g_mix_pre, l1_g_mix_post=l1_g_mix_post, l1_g_ff_pre=l1_g_ff_pre,
        l1_g_ff_post=l1_g_ff_post, l1_w_ff1=l1_w_ff1.astype(BF16), l1_w_ff2=l1_w_ff2.astype(BF16),
        l1_w_in=l1_w_in.astype(BF16), l1_w_out=l1_w_out.astype(BF16),
        pool_w=l1_pool_w.astype(BF16), l1_pool_scale=l1_pool_scale, l1_gmlp_ln_g=l1_gmlp_ln_g,
        l1_gmlp_ln_b=l1_gmlp_ln_b, gmlp_ws=l1_gmlp_ws.astype(BF16), gmlp_bs_t=l1_gmlp_bs.T,
    )

    zero_state = jnp.zeros((batch, 12, SUBLANES, LANES), F32)
    y_prompt, hfin = _trunk(x_prompt.reshape(batch * seq, D), None, batch, seq, lambda r: 0,
                            mods0, mods1, zero_state, p)
    new_re, new_im = _unpack_state(hfin)

    h0 = _pack_state(state_l0_s5_re, state_l0_s5_im)
    y_sample, _ = _trunk(x_sample.reshape(dec_batch * dec_seq, D), _pos_table(), dec_batch, dec_seq,
                         lambda r: 1 + r // dec_seq, mods0, mods1, h0, p)

    return (y_prompt.reshape(batch, seq, D), y_sample.reshape(dec_batch, dec_seq, D), new_re, new_im)
```

```python
import functools
import math

import jax
import jax.numpy as jnp
from jax import lax
from jax.experimental import pallas as pl
from jax.experimental.pallas import tpu as pltpu

F32 = jnp.float32
BF16 = jnp.bfloat16
I32 = jnp.int32

D = 1024
D_FF = 4 * D
N_MOD = 6
EPS = 1e-6
GRID_W = 64
POS_BASE = 10000.0

S5_CH = 768
S5_GROUPS = 48
S5_GROUP = 16
S5_STATE = 64
S5_CLUSTERS = 3
CL_GROUPS = 16
CL_CH = 256
CL_STATES = 1024
FNET_CH = 256
FNET_GW = 64

POOL_WINDOWS = (2, 4, 8, 16)
POOL_CH = 512
POOL_GW = 128
POOL_HALO = 8
GMLP_CH = 512
GMLP_HEADS = 4
GMLP_HD = 128
GMLP_CHUNK = 128

LANES = 128
SUBLANES = 8
VMEM_LIMIT = 56 * 1024 * 1024

S5_TT = 256
TM = 512
ROW_SPLIT = 2
FF_CHUNKS = 4


def _cparams(sem):
    return pltpu.CompilerParams(dimension_semantics=sem, vmem_limit_bytes=VMEM_LIMIT)


def _resident(shape):
    nd = len(shape)
    return pl.BlockSpec(shape, lambda *_: (0,) * nd, pipeline_mode=pl.Buffered(1))


def _rows(width, tm=TM):
    return pl.BlockSpec((tm, width), lambda i: (i, 0))


def _mod_spec(cond_map, tm=TM):
    return pl.BlockSpec((1, 1, N_MOD * D), lambda i: (cond_map(i * tm), 0, 0))


def _rms(x, g):
    return x * lax.rsqrt(jnp.mean(x * x, axis=-1, keepdims=True) + EPS) * g


def _mod(m, idx):
    return m[:, idx * D:(idx + 1) * D]


def _ffn_sublayer(x, m, gpre, gpost, w1_ref, w2_ref, a_ref, rows):
    h = (_rms(x, gpre) * (1.0 + _mod(m, 4)) + _mod(m, 3)).astype(BF16)
    cw = D_FF // FF_CHUNKS
    for f in range(FF_CHUNKS):
        cols = slice(f * cw, (f + 1) * cw)
        a = jnp.dot(h, w1_ref[:, cols], preferred_element_type=F32)
        a_ref[rows, cols] = jnp.square(jnp.maximum(a, 0.0)).astype(BF16)
    y = jnp.dot(a_ref[rows, :], w2_ref[...], preferred_element_type=F32)
    return x + _mod(m, 5) * _rms(y, gpost)


def _row_parts(tm):
    part = tm // ROW_SPLIT
    return [slice(s * part, (s + 1) * part) for s in range(ROW_SPLIT)]


def _mod_kernel(c_ref, w_ref, b_ref, o_ref):
    c = c_ref[...]
    s = c * jax.nn.sigmoid(c)
    o_ref[...] = jnp.dot(s.astype(BF16), w_ref[...].astype(BF16),
                         preferred_element_type=F32) + b_ref[...]


def _modulation(conds, w_mod, b_mod):
    tn = 1024
    out = pl.pallas_call(
        _mod_kernel,
        out_shape=jax.ShapeDtypeStruct((8, N_MOD * D), F32),
        grid=(N_MOD * D // tn,),
        in_specs=[pl.BlockSpec((8, D), lambda j: (0, 0)),
                  pl.BlockSpec((D, tn), lambda j: (0, j)),
                  pl.BlockSpec((1, tn), lambda j: (0, j))],
        out_specs=pl.BlockSpec((8, tn), lambda j: (0, j)),
        compiler_params=_cparams(("parallel",)),
        name="modulation",
    )(conds, w_mod, b_mod.reshape(1, -1))
    return out.reshape(8, 1, N_MOD * D)


def _pos_table_kernel(o_ref):
    quarter = D // 4
    k = lax.broadcasted_iota(I32, (GRID_W, quarter), 1).astype(F32)
    p = lax.broadcasted_iota(I32, (GRID_W, quarter), 0).astype(F32)
    ang = p * jnp.exp(k * (-math.log(POS_BASE) / quarter))
    o_ref[:, 0:quarter] = jnp.sin(ang)
    o_ref[:, quarter:] = jnp.cos(ang)


def _pos_table():
    return pl.pallas_call(
        _pos_table_kernel,
        out_shape=jax.ShapeDtypeStruct((GRID_W, D // 2), F32),
        name="pos_table",
    )()


def _add_pos(x_ref, tab_ref, xs_ref, tm):
    i = pl.program_id(0)
    half = D // 2
    row0 = (i * (tm // GRID_W)) % GRID_W
    for r in range(tm // GRID_W):
        sl = slice(r * GRID_W, (r + 1) * GRID_W)
        xs_ref[sl, 0:half] = x_ref[sl, 0:half] + tab_ref[pl.ds(row0 + r, 1), :]
        xs_ref[sl, half:] = x_ref[sl, half:] + tab_ref[...]
    return xs_ref[...]


def _inproj_kernel(*refs, with_pos, tm):
    if with_pos:
        x_ref, tab_ref, g_ref, m_ref, w_ref, o_ref, xs_ref = refs
        x = _add_pos(x_ref, tab_ref, xs_ref, tm)
    else:
        x_ref, g_ref, m_ref, w_ref, o_ref = refs
        x = x_ref[...]
    m = m_ref[0]
    h = _rms(x, g_ref[...]) * (1.0 + _mod(m, 1)) + _mod(m, 0)
    o_ref[...] = jnp.dot(h.astype(BF16), w_ref[...], preferred_element_type=F32)


def _inproj(x, tab, g, mods, cond_map, w_bf16, tm=TM):
    n = x.shape[0]
    nout = w_bf16.shape[1]
    with_pos = tab is not None
    args = [x] + ([tab] if with_pos else []) + [g.reshape(1, D), mods, w_bf16]
    in_specs = [_rows(D, tm)] + ([_resident(tab.shape)] if with_pos else [])
    in_specs += [_resident((1, D)), _mod_spec(cond_map, tm), _resident(w_bf16.shape)]
    return pl.pallas_call(
        functools.partial(_inproj_kernel, with_pos=with_pos, tm=tm),
        out_shape=jax.ShapeDtypeStruct((n, nout), F32),
        grid=(n // tm,),
        in_specs=in_specs,
        out_specs=_rows(nout, tm),
        scratch_shapes=[pltpu.VMEM((tm, D), F32)] if with_pos else [],
        compiler_params=_cparams(("parallel",)),
        name="in_proj0",
    )(*args)


def _s5_params_kernel(lre_ref, lim_ref, ldt_ref, bre_ref, bim_ref, cre_ref, cim_ref,
                      are_ref, aim_ref, bblk_ref, cblk_ref):
    lam_re = lre_ref[...]
    lam_im = lim_ref[...]
    dt = jnp.exp(ldt_ref[...])
    mag = jnp.exp(lam_re * dt)
    ab_re = mag * jnp.cos(lam_im * dt)
    ab_im = mag * jnp.sin(lam_im * dt)
    num_re = ab_re - 1.0
    num_im = ab_im
    den = lam_re * lam_re + lam_im * lam_im
    coef_re = (num_re * lam_re + num_im * lam_im) / den
    coef_im = (num_im * lam_re - num_re * lam_im) / den
    b_re = bre_ref[...]
    b_im = bim_ref[...]
    are_ref[...] = ab_re
    aim_ref[...] = ab_im
    bb = (coef_re * b_re - coef_im * b_im, coef_re * b_im + coef_im * b_re)

    bblk_ref[...] = jnp.zeros_like(bblk_ref)
    cblk_ref[...] = jnp.zeros_like(cblk_ref)

    pr = 2 * S5_GROUP
    r = lax.broadcasted_iota(I32, (pr, LANES), 0)
    l = lax.broadcasted_iota(I32, (pr, LANES), 1)
    pair_mask = (r < S5_GROUP) == (l < S5_STATE)
    for d in range(2):
        for c in range(S5_CLUSTERS):
            for ri in range(2):
                for mth in range(CL_GROUPS // 2):
                    r0 = ((d * S5_GROUPS + c * CL_GROUPS + 2 * mth) * S5_GROUP)
                    tile = jnp.where(pair_mask, bb[ri][r0:r0 + pr, :], 0.0)
                    col = (ri * SUBLANES + mth) * LANES
                    bblk_ref[d, c, mth * pr:(mth + 1) * pr, col:col + LANES] = tile.astype(BF16)

    gr = SUBLANES * S5_STATE
    r = lax.broadcasted_iota(I32, (gr, LANES), 0)
    l = lax.broadcasted_iota(I32, (gr, LANES), 1)
    oct_mask = (r >> 6) == (l >> 4)
    for d in range(2):
        for c in range(S5_CLUSTERS):
            for ri, c_ref in enumerate((cre_ref, cim_ref)):
                for n in range(CL_GROUPS // SUBLANES):
                    s0 = (d * S5_GROUPS + c * CL_GROUPS + SUBLANES * n) * S5_STATE
                    v = c_ref[s0:s0 + gr, :]
                    tile = jnp.where(oct_mask, v if ri == 0 else -v, 0.0)
                    row = ri * CL_STATES + n * gr
                    cblk_ref[d, c, row:row + gr, n * LANES:(n + 1) * LANES] = tile.astype(BF16)


def _s5_params(lam_re, lam_im, log_dt, b_re, b_im, c_re, c_im):
    rows = 2 * S5_GROUPS * S5_GROUP
    shp = (2, S5_GROUPS, S5_GROUP, S5_STATE)
    dup = lambda a: jnp.tile(a.reshape(rows, S5_STATE), (1, 2))
    rep = lambda a: dup(jnp.broadcast_to(a[:, :, None, :], shp))
    ldt = dup(jnp.broadcast_to(log_dt[:, :, None, None], shp))
    tr_b = lambda b: dup(jnp.swapaxes(b, 2, 3))
    crow = 2 * S5_GROUPS * S5_STATE
    tr_c = lambda c: jnp.tile(jnp.swapaxes(c, 2, 3).reshape(crow, S5_GROUP), (1, SUBLANES))
    ab_re, ab_im, b_blk, c_blk = pl.pallas_call(
        _s5_params_kernel,
        out_shape=[jax.ShapeDtypeStruct((rows, LANES), F32),
                   jax.ShapeDtypeStruct((rows, LANES), F32),
                   jax.ShapeDtypeStruct((2, S5_CLUSTERS, CL_CH, 2 * CL_STATES), BF16),
                   jax.ShapeDtypeStruct((2, S5_CLUSTERS, 2 * CL_STATES, CL_CH), BF16)],
        compiler_params=pltpu.CompilerParams(vmem_limit_bytes=VMEM_LIMIT),
        name="s5_params",
    )(rep(lam_re), rep(lam_im), ldt, tr_b(b_re), tr_b(b_im), tr_c(c_re), tr_c(c_im))
    pick = lambda a: a.reshape(2, S5_GROUPS, S5_GROUP, LANES)[:, :, 0, :S5_STATE]
    a_vec = jnp.stack([pick(ab_re), pick(ab_im)], axis=0)
    a_vec = a_vec.reshape(2, 2, S5_CLUSTERS, SUBLANES, LANES)
    a_vec = jnp.transpose(a_vec, (1, 2, 0, 3, 4)).reshape(12, SUBLANES, LANES)
    return b_blk, c_blk, a_vec


def _pack_state(s_re, s_im):
    b = s_re.shape[0]
    st = jnp.stack([s_re, s_im], axis=2)
    st = st.reshape(b, 2, 2, S5_CLUSTERS, SUBLANES, LANES)
    return jnp.transpose(st, (0, 1, 3, 2, 4, 5)).reshape(b, 12, SUBLANES, LANES)


def _unpack_state(st):
    b = st.shape[0]
    st = st.reshape(b, 2, S5_CLUSTERS, 2, SUBLANES, LANES)
    st = jnp.transpose(st, (0, 3, 1, 2, 4, 5)).reshape(b, 2, 2, S5_GROUPS, S5_STATE)
    return st[:, 0], st[:, 1]


def _s5_kernel(uf_ref, ub_ref, h0_ref, a_ref, bblk_ref, cblk_ref, yf_ref, yb_ref, hf_ref,
               s_ref, h_ref, st_ref, *, tt):
    j = pl.program_id(1)
    nj = pl.num_programs(1)
    half = tt // 2
    im0 = half * SUBLANES
    n_chain = 2 * S5_CLUSTERS

    @pl.when(j == 0)
    def _():
        st_ref[...] = h0_ref[0]

    def block_rows(d, p):
        first = p if d == 0 else 1 - p
        return slice(first * half, (first + 1) * half)

    u_refs = (uf_ref, ub_ref)
    y_refs = (yf_ref, yb_ref)

    def project_in(p, k):
        d, c = divmod(k, S5_CLUSTERS)
        u = u_refs[d][block_rows(d, p), c * CL_CH:(c + 1) * CL_CH].astype(BF16)
        res = jnp.dot(u, bblk_ref[d, c], preferred_element_type=F32)
        for q in range(SUBLANES):
            s_ref[p, k, pl.ds(q, half, stride=SUBLANES), :] = res[:, q * LANES:(q + 1) * LANES]
            s_ref[p, k, pl.ds(im0 + q, half, stride=SUBLANES), :] = (
                res[:, CL_STATES + q * LANES:CL_STATES + (q + 1) * LANES])

    def scan(p, state):
        for t in range(half):
            for k in range(n_chain):
                row = (t if k < S5_CLUSTERS else half - 1 - t) * SUBLANES
                hr, hi = state[2 * k], state[2 * k + 1]
                ar, ai = a[2 * k], a[2 * k + 1]
                nr = ar * hr - ai * hi + s_ref[p, k, row:row + SUBLANES, :]
                ni = ar * hi + ai * hr + s_ref[p, k, im0 + row:im0 + row + SUBLANES, :]
                h_ref[p, k, row:row + SUBLANES, :] = nr
                h_ref[p, k, im0 + row:im0 + row + SUBLANES, :] = ni
                state[2 * k], state[2 * k + 1] = nr, ni

    def project_out(p, k):
        d, c = divmod(k, S5_CLUSTERS)
        hcat = jnp.concatenate(
            [h_ref[p, k, pl.ds(base + q, half, stride=SUBLANES), :].astype(BF16)
             for base in (0, im0) for q in range(SUBLANES)], axis=-1)
        y_refs[d][block_rows(d, p), c * CL_CH:(c + 1) * CL_CH] = jnp.dot(
            hcat, cblk_ref[d, c], preferred_element_type=F32)

    a = [a_ref[i] for i in range(2 * n_chain)]
    state = [st_ref[i] for i in range(2 * n_chain)]
    for p in range(2):
        for k in range(n_chain):
            project_in(p, k)
    for p in range(2):
        scan(p, state)
        for k in range(n_chain):
            project_out(p, k)
    for i in range(2 * n_chain):
        st_ref[i] = state[i]

    @pl.when(j == nj - 1)
    def _():
        hf_ref[0] = st_ref[...]


def _s5_scan(z, n_seq, seq_len, h0, a_vec, b_blk, c_blk):
    tt = S5_TT
    nj = seq_len // tt
    n = n_seq * seq_len
    st_spec = pl.BlockSpec((1, 12, SUBLANES, LANES), lambda s, j: (s, 0, 0, 0))
    return pl.pallas_call(
        functools.partial(_s5_kernel, tt=tt),
        out_shape=[jax.ShapeDtypeStruct((n, S5_CH), F32),
                   jax.ShapeDtypeStruct((n, S5_CH), F32),
                   jax.ShapeDtypeStruct((n_seq, 12, SUBLANES, LANES), F32)],
        grid=(n_seq, nj),
        in_specs=[pl.BlockSpec((tt, S5_CH), lambda s, j: (s * nj + j, 0)),
                  pl.BlockSpec((tt, S5_CH), lambda s, j: (s * nj + nj - 1 - j, 0)),
                  st_spec,
                  _resident((12, SUBLANES, LANES)),
                  _resident(b_blk.shape),
                  _resident(c_blk.shape)],
        out_specs=[pl.BlockSpec((tt, S5_CH), lambda s, j: (s * nj + j, 0)),
                   pl.BlockSpec((tt, S5_CH), lambda s, j: (s * nj + nj - 1 - j, 0)),
                   st_spec],
        scratch_shapes=[pltpu.VMEM((2, 2 * S5_CLUSTERS, SUBLANES * tt, LANES), F32),
                        pltpu.VMEM((2, 2 * S5_CLUSTERS, SUBLANES * tt, LANES), F32),
                        pltpu.VMEM((12, SUBLANES, LANES), F32)],
        compiler_params=_cparams(("arbitrary", "arbitrary")),
        name="s5_scan",
    )(z, z, h0, a_vec, b_blk, c_blk)


def _fnet_kernel(u_ref, wf_ref, bf_ref, o_ref, cs_ref, cb_ref, sb_ref, vc_ref, vs_ref,
                 *, seq_len, tm, kc):
    s = pl.program_id(0)
    i = pl.program_id(1)
    two_pi = 2.0 * math.pi

    @pl.when(jnp.logical_and(s == 0, i == 0))
    def _():
        r = lax.broadcasted_iota(I32, (FNET_CH, FNET_CH), 0)
        k = lax.broadcasted_iota(I32, (FNET_CH, FNET_CH), 1)
        same = (r >> 6) == (k >> 6)
        ang = (((r & 63) * (k & 63)) & 63).astype(F32) * (two_pi / FNET_GW)
        cs_ref[:, 0:FNET_CH] = jnp.where(same, jnp.cos(ang), 0.0).astype(BF16)
        cs_ref[:, FNET_CH:] = jnp.where(same, jnp.sin(ang), 0.0).astype(BF16)

        def tab(rb, carry):
            r0 = pl.multiple_of(rb * SUBLANES, SUBLANES)
            j0 = lax.broadcasted_iota(I32, (SUBLANES, seq_len), 0) + r0
            kk = lax.broadcasted_iota(I32, (SUBLANES, seq_len), 1)
            a = ((j0 * kk) & (seq_len - 1)).astype(F32) * (two_pi / seq_len)
            cb_ref[pl.ds(r0, SUBLANES), :] = jnp.cos(a)
            sb_ref[pl.ds(r0, SUBLANES), :] = jnp.sin(a)
            return carry

        lax.fori_loop(0, tm // SUBLANES, tab, 0)

    @pl.when(i == 0)
    def _():
        rc = min(seq_len, 512)
        for r in range(seq_len // rc):
            sl = slice(r * rc, (r + 1) * rc)
            v = jnp.dot(u_ref[sl, :].astype(BF16), cs_ref[...], preferred_element_type=F32)
            vc_ref[sl, :] = v[:, 0:FNET_CH].astype(BF16)
            vs_ref[sl, :] = v[:, FNET_CH:].astype(BF16)

    kk = lax.broadcasted_iota(I32, (1, seq_len), 1)
    ang = (((i * tm) * kk) & (seq_len - 1)).astype(F32) * (two_pi / seq_len)
    ca = jnp.cos(ang)
    sa = jnp.sin(ang)
    acc = jnp.zeros((tm, FNET_CH), F32)
    for kci in range(seq_len // kc):
        sl = slice(kci * kc, (kci + 1) * kc)
        cb = cb_ref[:, sl]
        sb = sb_ref[:, sl]
        cmat = (ca[:, sl] * cb - sa[:, sl] * sb).astype(BF16)
        smat = (sa[:, sl] * cb + ca[:, sl] * sb).astype(BF16)
        acc = acc + jnp.dot(cmat, vc_ref[sl, :], preferred_element_type=F32)
        acc = acc - jnp.dot(smat, vs_ref[sl, :], preferred_element_type=F32)
    zf = acc * (1.0 / math.sqrt(seq_len * FNET_GW))
    o_ref[...] = jnp.dot(zf.astype(BF16), wf_ref[...], preferred_element_type=F32) + bf_ref[...]


def _fnet(z, n_seq, seq_len, w_blk, b_vec):
    tm = 256
    kc = min(seq_len, 1024)
    ni = seq_len // tm
    n = n_seq * seq_len
    col = (D - FNET_CH) // FNET_CH
    return pl.pallas_call(
        functools.partial(_fnet_kernel, seq_len=seq_len, tm=tm, kc=kc),
        out_shape=jax.ShapeDtypeStruct((n, FNET_CH), F32),
        grid=(n_seq, ni),
        in_specs=[pl.BlockSpec((seq_len, FNET_CH), lambda s, i: (s, col)),
                  pl.BlockSpec((FNET_CH, FNET_CH), lambda s, i: (0, 0)),
                  pl.BlockSpec((1, FNET_CH), lambda s, i: (0, 0))],
        out_specs=pl.BlockSpec((tm, FNET_CH), lambda s, i: (s * ni + i, 0)),
        scratch_shapes=[pltpu.VMEM((FNET_CH, 2 * FNET_CH), BF16),
                        pltpu.VMEM((tm, seq_len), F32),
                        pltpu.VMEM((tm, seq_len), F32),
                        pltpu.VMEM((seq_len, FNET_CH), BF16),
                        pltpu.VMEM((seq_len, FNET_CH), BF16)],
        compiler_params=_cparams(("arbitrary", "arbitrary")),
        name="fnet",
    )(z, w_blk, b_vec)


def _post0_kernel(*refs, with_pos, tm):
    if with_pos:
        (yf_ref, yb_ref, u_ref, fn_ref, x_ref, tab_ref, d_ref, wglu_ref, bglu_ref, woa_ref,
         wob_ref, gpost_ref, gfpre_ref, gfpost_ref, m_ref, w1_ref, w2_ref, o_ref, a_ref, xs_ref) = refs
        _add_pos(x_ref, tab_ref, xs_ref, tm)
        x_ref = xs_ref
    else:
        (yf_ref, yb_ref, u_ref, fn_ref, x_ref, d_ref, wglu_ref, bglu_ref, woa_ref,
         wob_ref, gpost_ref, gfpre_ref, gfpost_ref, m_ref, w1_ref, w2_ref, o_ref, a_ref) = refs
    m = m_ref[0]
    for rows in _row_parts(tm):
        y = yf_ref[rows, :] + yb_ref[rows, :] + d_ref[...] * u_ref[rows, :]
        g = jax.nn.gelu(y)
        gate = jnp.dot(g.astype(BF16), wglu_ref[...], preferred_element_type=F32) + bglu_ref[...]
        ya = g * jax.nn.sigmoid(gate)
        o = jnp.dot(ya.astype(BF16), woa_ref[...], preferred_element_type=F32)
        o = o + jnp.dot(fn_ref[rows, :].astype(BF16), wob_ref[...], preferred_element_type=F32)
        x = x_ref[rows, :] + _mod(m, 2) * _rms(o, gpost_ref[...])
        o_ref[rows, :] = _ffn_sublayer(x, m, gfpre_ref[...], gfpost_ref[...], w1_ref, w2_ref,
                                       a_ref, rows)


def _post0(yf, yb, z, yfn, x, tab, mods, cond_map, p, tm=TM):
    n = x.shape[0]
    with_pos = tab is not None
    vec = lambda a, w: a.reshape(1, w)
    args = [yf, yb, z, yfn, x] + ([tab] if with_pos else [])
    in_specs = [_rows(S5_CH, tm)] * 3 + [_rows(FNET_CH, tm), _rows(D, tm)]
    in_specs += [_resident(tab.shape)] if with_pos else []
    consts = [vec(p["l0_s5_d"], S5_CH), p["l0_s5_w_glu"], vec(p["l0_s5_b_glu"], S5_CH),
              p["l0_w_out_a"], p["l0_w_out_b"], vec(p["l0_g_mix_post"], D),
              vec(p["l0_g_ff_pre"], D), vec(p["l0_g_ff_post"], D)]
    args += consts + [mods, p["l0_w_ff1"], p["l0_w_ff2"]]
    in_specs += [_resident(a.shape) for a in consts] + [_mod_spec(cond_map, tm),
                                                        _resident((D, D_FF)), _resident((D_FF, D))]
    scratch = [pltpu.VMEM((tm, D_FF), BF16)] + ([pltpu.VMEM((tm, D), F32)] if with_pos else [])
    return pl.pallas_call(
        functools.partial(_post0_kernel, with_pos=with_pos, tm=tm),
        out_shape=jax.ShapeDtypeStruct((n, D), F32),
        grid=(n // tm,),
        in_specs=in_specs,
        out_specs=_rows(D, tm),
        scratch_shapes=scratch,
        compiler_params=_cparams(("parallel",)),
        name="post0",
    )(*args)


def _layer1_kernel(x_ref, xp_ref, xn_ref, gpre_ref, m_ref, win_ref, pw_ref, ps_ref, lng_ref,
                   lnb_ref, ws_ref, bs_ref, wout_ref, gpost_ref, gfpre_ref, gfpost_ref,
                   w1_ref, w2_ref, o_ref, xe_ref, y_ref, a_ref, *, tm, seq_len):
    i = pl.program_id(0)
    hl = POOL_HALO
    m = m_ref[0]
    x = x_ref[...]
    scale = 1.0 + _mod(m, 1)
    shift = _mod(m, 0)
    h = (_rms(x, gpre_ref[...]) * scale + shift).astype(BF16)

    r0 = i * tm
    prev_ok = (r0 & (seq_len - 1)) != 0
    next_ok = ((r0 + tm) & (seq_len - 1)) != 0
    xh = jnp.concatenate([xp_ref[...], xn_ref[...]], axis=0)
    hh = (_rms(xh, gpre_ref[...]) * scale + shift).astype(BF16)
    zh = jnp.dot(hh, win_ref[:, 0:POOL_CH], preferred_element_type=F32)
    xe_ref[0:hl, :] = jnp.where(prev_ok, zh[0:hl], 0.0)
    xe_ref[hl + tm:hl + tm + hl, :] = jnp.where(next_ok, zh[hl:], 0.0)
    xe_ref[hl:hl + tm, :] = jnp.dot(h, win_ref[:, 0:POOL_CH], preferred_element_type=F32)

    for rows in _row_parts(tm):
        part = rows.stop - rows.start
        t = (r0 + rows.start + lax.broadcasted_iota(I32, (part, 1), 0)) & (seq_len - 1)
        for gi, win in enumerate(POOL_WINDOWS):
            half = win // 2
            cols = slice(gi * POOL_GW, (gi + 1) * POOL_GW)
            acc = None
            for o in range(-half, half):
                v = xe_ref[pl.ds(hl + rows.start + o, part), cols]
                if o != 0:
                    v = jnp.where((t >= -o) if o < 0 else (t < seq_len - o), v, 0.0)
                acc = v if acc is None else acc + v
            cnt = (jnp.minimum(t + half, seq_len) - jnp.maximum(t - half, 0)).astype(F32)
            pooled = acc / cnt - xe_ref[pl.ds(hl + rows.start, part), cols]
            mixed = jnp.dot(pooled.astype(BF16), pw_ref[gi], preferred_element_type=F32)
            y_ref[rows, cols] = (mixed * ps_ref[:, cols]).astype(BF16)

        hp = h[rows, :]
        zu = jax.nn.gelu(jnp.dot(hp, win_ref[:, POOL_CH:POOL_CH + GMLP_CH],
                                 preferred_element_type=F32))
        zv = jax.nn.gelu(jnp.dot(hp, win_ref[:, POOL_CH + GMLP_CH:], preferred_element_type=F32))
        for hd in range(GMLP_HEADS):
            hc = slice(hd * GMLP_HD, (hd + 1) * GMLP_HD)
            v = zv[:, hc]
            vc = v - jnp.mean(v, axis=-1, keepdims=True)
            var = jnp.mean(vc * vc, axis=-1, keepdims=True)
            vn = (vc * lax.rsqrt(var + EPS) * lng_ref[:, hc] + lnb_ref[:, hc]).astype(BF16)
            for ch in range(part // GMLP_CHUNK):
                cr = slice(ch * GMLP_CHUNK, (ch + 1) * GMLP_CHUNK)
                sv = (jnp.dot(ws_ref[hd], vn[cr, :], preferred_element_type=F32)
                      + bs_ref[:, hd:hd + 1])
                y_ref[pl.ds(rows.start + ch * GMLP_CHUNK, GMLP_CHUNK),
                      POOL_CH + hd * GMLP_HD:POOL_CH + (hd + 1) * GMLP_HD] = (
                    zu[cr, hc] * sv).astype(BF16)

        o = jnp.dot(y_ref[rows, :], wout_ref[...], preferred_element_type=F32)
        xm = x[rows, :] + _mod(m, 2) * _rms(o, gpost_ref[...])
        o_ref[rows, :] = _ffn_sublayer(xm, m, gfpre_ref[...], gfpost_ref[...], w1_ref, w2_ref,
                                       a_ref, rows)


def _layer1(x, seq_len, mods, cond_map, p, tm=TM):
    n = x.shape[0]
    hb = tm // POOL_HALO
    nhb = n // POOL_HALO
    vec = lambda a, w: a.reshape(1, w)
    consts_a = [vec(p["l1_g_mix_pre"], D)]
    consts_b = [p["l1_w_in"], p["pool_w"], vec(p["l1_pool_scale"], POOL_CH),
                vec(p["l1_gmlp_ln_g"], GMLP_CH), vec(p["l1_gmlp_ln_b"], GMLP_CH), p["gmlp_ws"],
                p["gmlp_bs_t"], p["l1_w_out"], vec(p["l1_g_mix_post"], D),
                vec(p["l1_g_ff_pre"], D), vec(p["l1_g_ff_post"], D), p["l1_w_ff1"], p["l1_w_ff2"]]
    in_specs = [_rows(D, tm),
                pl.BlockSpec((POOL_HALO, D), lambda i: (jnp.maximum(i * hb - 1, 0), 0)),
                pl.BlockSpec((POOL_HALO, D), lambda i: (jnp.minimum((i + 1) * hb, nhb - 1), 0))]
    in_specs += [_resident(a.shape) for a in consts_a] + [_mod_spec(cond_map, tm)]
    in_specs += [_resident(a.shape) for a in consts_b]
    return pl.pallas_call(
        functools.partial(_layer1_kernel, tm=tm, seq_len=seq_len),
        out_shape=jax.ShapeDtypeStruct((n, D), F32),
        grid=(n // tm,),
        in_specs=in_specs,
        out_specs=_rows(D, tm),
        scratch_shapes=[pltpu.VMEM((tm + 2 * POOL_HALO, POOL_CH), F32),
                        pltpu.VMEM((tm, D), BF16),
                        pltpu.VMEM((tm, D_FF), BF16)],
        compiler_params=_cparams(("parallel",)),
        name="layer1",
    )(x, x, x, *consts_a, mods, *consts_b)


def _trunk(x, tab, n_seq, seq_len, cond_map, mods0, mods1, h0, p):
    z = _inproj(x, tab, p["l0_g_mix_pre"], mods0, cond_map, p["l0_w_in"])
    yf, yb, hfin = _s5_scan(z, n_seq, seq_len, h0, p["a_vec"], p["b_blk"], p["c_blk"])
    yfn = _fnet(z, n_seq, seq_len, p["fnet_w_blk"], p["fnet_b"])
    x = _post0(yf, yb, z, yfn, x, tab, mods0, cond_map, p)
    x = _layer1(x, seq_len, mods1, cond_map, p)
    return x, hfin


def kernel(x_prompt, x_sample, state_l0_s5_re, state_l0_s5_im, c, c_ctx, l0_w_mod, l0_b_mod, l0_g_mix_pre, l0_g_mix_post, l0_g_ff_pre, l0_g_ff_post, l0_w_ff1, l0_w_ff2, l0_w_in, l0_w_out, l0_s5_lambda_re, l0_s5_lambda_im, l0_s5_log_dt, l0_s5_b_re, l0_s5_b_im, l0_s5_c_re, l0_s5_c_im, l0_s5_d, l0_s5_w_glu, l0_s5_b_glu, l0_fnet_w, l0_fnet_b, l1_w_mod, l1_b_mod, l1_g_mix_pre, l1_g_mix_post, l1_g_ff_pre, l1_g_ff_post, l1_w_ff1, l1_w_ff2, l1_w_in, l1_w_out, l1_pool_w, l1_pool_scale, l1_gmlp_ln_g, l1_gmlp_ln_b, l1_gmlp_ws, l1_gmlp_bs):
    batch, seq, _ = x_prompt.shape
    dec_batch, dec_seq, _ = x_sample.shape
    assert dec_seq == GRID_W * GRID_W and seq % S5_TT == 0 and TM % seq == 0

    conds = jnp.zeros((8, D), F32).at[0].set(c_ctx).at[1:1 + dec_batch].set(c)
    mods0 = _modulation(conds, l0_w_mod, l0_b_mod)
    mods1 = _modulation(conds, l1_w_mod, l1_b_mod)

    b_blk, c_blk, a_vec = _s5_params(l0_s5_lambda_re, l0_s5_lambda_im, l0_s5_log_dt,
                                     l0_s5_b_re, l0_s5_b_im, l0_s5_c_re, l0_s5_c_im)

    eye4 = jnp.eye(FNET_CH // FNET_GW, dtype=F32)
    fnet_w_blk = jnp.einsum("gcd,gk->gckd", l0_fnet_w, eye4).reshape(FNET_CH, FNET_CH).astype(BF16)

    p = dict(
        l0_g_mix_pre=l0_g_mix_pre, l0_g_mix_post=l0_g_mix_post, l0_g_ff_pre=l0_g_ff_pre,
        l0_g_ff_post=l0_g_ff_post, l0_w_ff1=l0_w_ff1.astype(BF16), l0_w_ff2=l0_w_ff2.astype(BF16),
        l0_w_in=l0_w_in.astype(BF16), l0_w_out_a=l0_w_out[:S5_CH].astype(BF16),
        l0_w_out_b=l0_w_out[S5_CH:].astype(BF16), l0_s5_d=l0_s5_d,
        l0_s5_w_glu=l0_s5_w_glu.astype(BF16), l0_s5_b_glu=l0_s5_b_glu,
        a_vec=a_vec, b_blk=b_blk, c_blk=c_blk, fnet_w_blk=fnet_w_blk,
        fnet_b=l0_fnet_b.reshape(1, FNET_CH),
        l1_g_mix_pre=l1_g_mix_pre, l1_g_mix_post=l1_g_mix_post, l1_g_ff_pre=l1_g_ff_pre,
        l1_g_ff_post=l1_g_ff_post, l1_w_ff1=l1_w_ff1.astype(BF16), l1_w_ff2=l1_w_ff2.astype(BF16),
        l1_w_in=l1_w_in.astype(BF16), l1_w_out=l1_w_out.astype(BF16),
        pool_w=l1_pool_w.astype(BF16), l1_pool_scale=l1_pool_scale, l1_gmlp_ln_g=l1_gmlp_ln_g,
        l1_gmlp_ln_b=l1_gmlp_ln_b, gmlp_ws=l1_gmlp_ws.astype(BF16), gmlp_bs_t=l1_gmlp_bs.T,
    )

    zero_state = jnp.zeros((batch, 12, SUBLANES, LANES), F32)
    y_prompt, hfin = _trunk(x_prompt.reshape(batch * seq, D), None, batch, seq, lambda r: 0,
                            mods0, mods1, zero_state, p)
    new_re, new_im = _unpack_state(hfin)

    h0 = _pack_state(state_l0_s5_re, state_l0_s5_im)
    y_sample, _ = _trunk(x_sample.reshape(dec_batch * dec_seq, D), _pos_table(), dec_batch, dec_seq,
                         lambda r: 1 + r // dec_seq, mods0, mods1, h0, p)

    return (y_prompt.reshape(batch, seq, D), y_sample.reshape(dec_batch, dec_seq, D), new_re, new_im)
```

```python
import functools
import math

import jax
import jax.numpy as jnp
from jax import lax
from jax.experimental import pallas as pl
from jax.experimental.pallas import tpu as pltpu

F32 = jnp.float32
BF16 = jnp.bfloat16
I32 = jnp.int32

D = 1024
D_FF = 4 * D
N_MOD = 6
EPS = 1e-6
GRID_W = 64
POS_BASE = 10000.0

S5_CH = 768
S5_GROUPS = 48
S5_GROUP = 16
S5_STATE = 64
S5_CLUSTERS = 3
CL_GROUPS = 16
CL_CH = 256
CL_STATES = 1024
FNET_CH = 256
FNET_GW = 64

POOL_WINDOWS = (2, 4, 8, 16)
POOL_CH = 512
POOL_GW = 128
POOL_HALO = 8
GMLP_CH = 512
GMLP_HEADS = 4
GMLP_HD = 128
GMLP_CHUNK = 128

LANES = 128
SUBLANES = 8
VMEM_LIMIT = 56 * 1024 * 1024

S5_TT = 256
TM = 512
POST0_SPLIT = 2
LAYER1_SPLIT = 1
FF_CHUNKS = 4


def _cparams(sem):
    return pltpu.CompilerParams(dimension_semantics=sem, vmem_limit_bytes=VMEM_LIMIT)


def _resident(shape):
    nd = len(shape)
    return pl.BlockSpec(shape, lambda *_: (0,) * nd, pipeline_mode=pl.Buffered(1))


def _rows(width, tm=TM):
    return pl.BlockSpec((tm, width), lambda i: (i, 0))


def _mod_spec(cond_map, tm=TM):
    return pl.BlockSpec((1, 1, N_MOD * D), lambda i: (cond_map(i * tm), 0, 0))


def _rms(x, g):
    return x * lax.rsqrt(jnp.mean(x * x, axis=-1, keepdims=True) + EPS) * g


def _mod(m, idx):
    return m[:, idx * D:(idx + 1) * D]


def _ffn_sublayer(x, m, gpre, gpost, w1_ref, w2_ref, a_ref, rows):
    h = (_rms(x, gpre) * (1.0 + _mod(m, 4)) + _mod(m, 3)).astype(BF16)
    cw = D_FF // FF_CHUNKS
    for f in range(FF_CHUNKS):
        cols = slice(f * cw, (f + 1) * cw)
        a = jnp.dot(h, w1_ref[:, cols], preferred_element_type=F32)
        a_ref[rows, cols] = jnp.square(jnp.maximum(a, 0.0)).astype(BF16)
    y = jnp.dot(a_ref[rows, :], w2_ref[...], preferred_element_type=F32)
    return x + _mod(m, 5) * _rms(y, gpost)


def _row_parts(tm, n_parts):
    part = tm // n_parts
    return [slice(s * part, (s + 1) * part) for s in range(n_parts)]


def _mod_kernel(c_ref, w_ref, b_ref, o_ref):
    c = c_ref[...]
    s = c * jax.nn.sigmoid(c)
    o_ref[...] = jnp.dot(s.astype(BF16), w_ref[...].astype(BF16),
                         preferred_element_type=F32) + b_ref[...]


def _modulation(conds, w_mod, b_mod):
    tn = 1024
    out = pl.pallas_call(
        _mod_kernel,
        out_shape=jax.ShapeDtypeStruct((8, N_MOD * D), F32),
        grid=(N_MOD * D // tn,),
        in_specs=[pl.BlockSpec((8, D), lambda j: (0, 0)),
                  pl.BlockSpec((D, tn), lambda j: (0, j)),
                  pl.BlockSpec((1, tn), lambda j: (0, j))],
        out_specs=pl.BlockSpec((8, tn), lambda j: (0, j)),
        compiler_params=_cparams(("parallel",)),
        name="modulation",
    )(conds, w_mod, b_mod.reshape(1, -1))
    return out.reshape(8, 1, N_MOD * D)


def _pos_table_kernel(o_ref):
    quarter = D // 4
    k = lax.broadcasted_iota(I32, (GRID_W, quarter), 1).astype(F32)
    p = lax.broadcasted_iota(I32, (GRID_W, quarter), 0).astype(F32)
    ang = p * jnp.exp(k * (-math.log(POS_BASE) / quarter))
    o_ref[:, 0:quarter] = jnp.sin(ang)
    o_ref[:, quarter:] = jnp.cos(ang)


def _pos_table():
    return pl.pallas_call(
        _pos_table_kernel,
        out_shape=jax.ShapeDtypeStruct((GRID_W, D // 2), F32),
        name="pos_table",
    )()


def _add_pos(x_ref, tab_ref, xs_ref, tm):
    i = pl.program_id(0)
    half = D // 2
    row0 = (i * (tm // GRID_W)) % GRID_W
    for r in range(tm // GRID_W):
        sl = slice(r * GRID_W, (r + 1) * GRID_W)
        xs_ref[sl, 0:half] = x_ref[sl, 0:half] + tab_ref[pl.ds(row0 + r, 1), :]
        xs_ref[sl, half:] = x_ref[sl, half:] + tab_ref[...]
    return xs_ref[...]


def _inproj_kernel(*refs, with_pos, tm):
    if with_pos:
        x_ref, tab_ref, g_ref, m_ref, w_ref, o_ref, xs_ref = refs
        x = _add_pos(x_ref, tab_ref, xs_ref, tm)
    else:
        x_ref, g_ref, m_ref, w_ref, o_ref = refs
        x = x_ref[...]
    m = m_ref[0]
    h = _rms(x, g_ref[...]) * (1.0 + _mod(m, 1)) + _mod(m, 0)
    o_ref[...] = jnp.dot(h.astype(BF16), w_ref[...], preferred_element_type=F32)


def _inproj(x, tab, g, mods, cond_map, w_bf16, tm=TM):
    n = x.shape[0]
    nout = w_bf16.shape[1]
    with_pos = tab is not None
    args = [x] + ([tab] if with_pos else []) + [g.reshape(1, D), mods, w_bf16]
    in_specs = [_rows(D, tm)] + ([_resident(tab.shape)] if with_pos else [])
    in_specs += [_resident((1, D)), _mod_spec(cond_map, tm), _resident(w_bf16.shape)]
    return pl.pallas_call(
        functools.partial(_inproj_kernel, with_pos=with_pos, tm=tm),
        out_shape=jax.ShapeDtypeStruct((n, nout), F32),
        grid=(n // tm,),
        in_specs=in_specs,
        out_specs=_rows(nout, tm),
        scratch_shapes=[pltpu.VMEM((tm, D), F32)] if with_pos else [],
        compiler_params=_cparams(("parallel",)),
        name="in_proj0",
    )(*args)


def _s5_params_kernel(lre_ref, lim_ref, ldt_ref, bre_ref, bim_ref, cre_ref, cim_ref,
                      are_ref, aim_ref, bblk_ref, cblk_ref):
    lam_re = lre_ref[...]
    lam_im = lim_ref[...]
    dt = jnp.exp(ldt_ref[...])
    mag = jnp.exp(lam_re * dt)
    ab_re = mag * jnp.cos(lam_im * dt)
    ab_im = mag * jnp.sin(lam_im * dt)
    num_re = ab_re - 1.0
    num_im = ab_im
    den = lam_re * lam_re + lam_im * lam_im
    coef_re = (num_re * lam_re + num_im * lam_im) / den
    coef_im = (num_im * lam_re - num_re * lam_im) / den
    b_re = bre_ref[...]
    b_im = bim_ref[...]
    are_ref[...] = ab_re
    aim_ref[...] = ab_im
    bb = (coef_re * b_re - coef_im * b_im, coef_re * b_im + coef_im * b_re)

    bblk_ref[...] = jnp.zeros_like(bblk_ref)
    cblk_ref[...] = jnp.zeros_like(cblk_ref)

    pr = 2 * S5_GROUP
    r = lax.broadcasted_iota(I32, (pr, LANES), 0)
    l = lax.broadcasted_iota(I32, (pr, LANES), 1)
    pair_mask = (r < S5_GROUP) == (l < S5_STATE)
    for d in range(2):
        for c in range(S5_CLUSTERS):
            for ri in range(2):
                for mth in range(CL_GROUPS // 2):
                    r0 = ((d * S5_GROUPS + c * CL_GROUPS + 2 * mth) * S5_GROUP)
                    tile = jnp.where(pair_mask, bb[ri][r0:r0 + pr, :], 0.0)
                    col = (ri * SUBLANES + mth) * LANES
                    bblk_ref[d, c, mth * pr:(mth + 1) * pr, col:col + LANES] = tile.astype(BF16)

    gr = SUBLANES * S5_STATE
    r = lax.broadcasted_iota(I32, (gr, LANES), 0)
    l = lax.broadcasted_iota(I32, (gr, LANES), 1)
    oct_mask = (r >> 6) == (l >> 4)
    for d in range(2):
        for c in range(S5_CLUSTERS):
            for ri, c_ref in enumerate((cre_ref, cim_ref)):
                for n in range(CL_GROUPS // SUBLANES):
                    s0 = (d * S5_GROUPS + c * CL_GROUPS + SUBLANES * n) * S5_STATE
                    v = c_ref[s0:s0 + gr, :]
                    tile = jnp.where(oct_mask, v if ri == 0 else -v, 0.0)
                    row = ri * CL_STATES + n * gr
                    cblk_ref[d, c, row:row + gr, n * LANES:(n + 1) * LANES] = tile.astype(BF16)


def _s5_params(lam_re, lam_im, log_dt, b_re, b_im, c_re, c_im):
    rows = 2 * S5_GROUPS * S5_GROUP
    shp = (2, S5_GROUPS, S5_GROUP, S5_STATE)
    dup = lambda a: jnp.tile(a.reshape(rows, S5_STATE), (1, 2))
    rep = lambda a: dup(jnp.broadcast_to(a[:, :, None, :], shp))
    ldt = dup(jnp.broadcast_to(log_dt[:, :, None, None], shp))
    tr_b = lambda b: dup(jnp.swapaxes(b, 2, 3))
    crow = 2 * S5_GROUPS * S5_STATE
    tr_c = lambda c: jnp.tile(jnp.swapaxes(c, 2, 3).reshape(crow, S5_GROUP), (1, SUBLANES))
    ab_re, ab_im, b_blk, c_blk = pl.pallas_call(
        _s5_params_kernel,
        out_shape=[jax.ShapeDtypeStruct((rows, LANES), F32),
                   jax.ShapeDtypeStruct((rows, LANES), F32),
                   jax.ShapeDtypeStruct((2, S5_CLUSTERS, CL_CH, 2 * CL_STATES), BF16),
                   jax.ShapeDtypeStruct((2, S5_CLUSTERS, 2 * CL_STATES, CL_CH), BF16)],
        compiler_params=pltpu.CompilerParams(vmem_limit_bytes=VMEM_LIMIT),
        name="s5_params",
    )(rep(lam_re), rep(lam_im), ldt, tr_b(b_re), tr_b(b_im), tr_c(c_re), tr_c(c_im))
    pick = lambda a: a.reshape(2, S5_GROUPS, S5_GROUP, LANES)[:, :, 0, :S5_STATE]
    a_vec = jnp.stack([pick(ab_re), pick(ab_im)], axis=0)
    a_vec = a_vec.reshape(2, 2, S5_CLUSTERS, SUBLANES, LANES)
    a_vec = jnp.transpose(a_vec, (1, 2, 0, 3, 4)).reshape(12, SUBLANES, LANES)
    return b_blk, c_blk, a_vec


def _pack_state(s_re, s_im):
    b = s_re.shape[0]
    st = jnp.stack([s_re, s_im], axis=2)
    st = st.reshape(b, 2, 2, S5_CLUSTERS, SUBLANES, LANES)
    return jnp.transpose(st, (0, 1, 3, 2, 4, 5)).reshape(b, 12, SUBLANES, LANES)


def _unpack_state(st):
    b = st.shape[0]
    st = st.reshape(b, 2, S5_CLUSTERS, 2, SUBLANES, LANES)
    st = jnp.transpose(st, (0, 3, 1, 2, 4, 5)).reshape(b, 2, 2, S5_GROUPS, S5_STATE)
    return st[:, 0], st[:, 1]


def _s5_kernel(uf_ref, ub_ref, h0_ref, a_ref, bblk_ref, cblk_ref, yf_ref, yb_ref, hf_ref,
               s_ref, h_ref, st_ref, *, tt):
    j = pl.program_id(1)
    nj = pl.num_programs(1)
    half = tt // 2
    im0 = half * SUBLANES
    pitch = half + SUBLANES
    n_chain = 2 * S5_CLUSTERS

    @pl.when(j == 0)
    def _():
        st_ref[...] = h0_ref[0]

    def block_rows(d, p):
        first = p if d == 0 else 1 - p
        return slice(first * half, (first + 1) * half)

    u_refs = (uf_ref, ub_ref)
    y_refs = (yf_ref, yb_ref)

    def project_in(p, k):
        d, c = divmod(k, S5_CLUSTERS)
        u = u_refs[d][block_rows(d, p), c * CL_CH:(c + 1) * CL_CH].astype(BF16)
        res = jnp.dot(u, bblk_ref[d, c], preferred_element_type=F32)
        for q in range(SUBLANES):
            s_ref[p, k, pl.ds(q, half, stride=SUBLANES), :] = res[:, q * LANES:(q + 1) * LANES]
            s_ref[p, k, pl.ds(im0 + q, half, stride=SUBLANES), :] = (
                res[:, CL_STATES + q * LANES:CL_STATES + (q + 1) * LANES])

    def scan(p, state):
        for t in range(half):
            for k in range(n_chain):
                tl = t if k < S5_CLUSTERS else half - 1 - t
                row = tl * SUBLANES
                hr, hi = state[2 * k], state[2 * k + 1]
                ar, ai = a[2 * k], a[2 * k + 1]
                nr = ar * hr - ai * hi + s_ref[p, k, row:row + SUBLANES, :]
                ni = ar * hi + ai * hr + s_ref[p, k, im0 + row:im0 + row + SUBLANES, :]
                h_ref[p, k, pl.ds(tl, SUBLANES, stride=pitch), :] = nr
                h_ref[p, k, pl.ds(SUBLANES * pitch + tl, SUBLANES, stride=pitch), :] = ni
                state[2 * k], state[2 * k + 1] = nr, ni

    def project_out(p, k):
        d, c = divmod(k, S5_CLUSTERS)
        hcat = jnp.concatenate(
            [h_ref[p, k, q * pitch:q * pitch + half, :].astype(BF16)
             for q in range(2 * SUBLANES)], axis=-1)
        y_refs[d][block_rows(d, p), c * CL_CH:(c + 1) * CL_CH] = jnp.dot(
            hcat, cblk_ref[d, c], preferred_element_type=F32)

    a = [a_ref[i] for i in range(2 * n_chain)]
    state = [st_ref[i] for i in range(2 * n_chain)]
    for p in range(2):
        for k in range(n_chain):
            project_in(p, k)
    for p in range(2):
        scan(p, state)
        for k in range(n_chain):
            project_out(p, k)
    for i in range(2 * n_chain):
        st_ref[i] = state[i]

    @pl.when(j == nj - 1)
    def _():
        hf_ref[0] = st_ref[...]


def _s5_scan(z, n_seq, seq_len, h0, a_vec, b_blk, c_blk):
    tt = S5_TT
    slab_rows = 2 * SUBLANES * (tt // 2 + SUBLANES)
    nj = seq_len // tt
    n = n_seq * seq_len
    st_spec = pl.BlockSpec((1, 12, SUBLANES, LANES), lambda s, j: (s, 0, 0, 0))
    return pl.pallas_call(
        functools.partial(_s5_kernel, tt=tt),
        out_shape=[jax.ShapeDtypeStruct((n, S5_CH), F32),
                   jax.ShapeDtypeStruct((n, S5_CH), F32),
                   jax.ShapeDtypeStruct((n_seq, 12, SUBLANES, LANES), F32)],
        grid=(n_seq, nj),
        in_specs=[pl.BlockSpec((tt, S5_CH), lambda s, j: (s * nj + j, 0)),
                  pl.BlockSpec((tt, S5_CH), lambda s, j: (s * nj + nj - 1 - j, 0)),
                  st_spec,
                  _resident((12, SUBLANES, LANES)),
                  _resident(b_blk.shape),
                  _resident(c_blk.shape)],
        out_specs=[pl.BlockSpec((tt, S5_CH), lambda s, j: (s * nj + j, 0)),
                   pl.BlockSpec((tt, S5_CH), lambda s, j: (s * nj + nj - 1 - j, 0)),
                   st_spec],
        scratch_shapes=[pltpu.VMEM((2, 2 * S5_CLUSTERS, SUBLANES * tt, LANES), F32),
                        pltpu.VMEM((2, 2 * S5_CLUSTERS, slab_rows, LANES), F32),
                        pltpu.VMEM((12, SUBLANES, LANES), F32)],
        compiler_params=_cparams(("arbitrary", "arbitrary")),
        name="s5_scan",
    )(z, z, h0, a_vec, b_blk, c_blk)


def _fnet_kernel(u_ref, wf_ref, bf_ref, o_ref, cs_ref, j_ref, cb_ref, sb_ref, vc_ref, vs_ref,
                 w_ref, vmid_ref, *, seq_len, tm, kc, tr):
    s = pl.program_id(0)
    i = pl.program_id(1)
    two_pi = 2.0 * math.pi
    hk = seq_len // 2

    @pl.when(jnp.logical_and(s == 0, i == 0))
    def _():
        r = lax.broadcasted_iota(I32, (FNET_CH, FNET_CH), 0)
        k = lax.broadcasted_iota(I32, (FNET_CH, FNET_CH), 1)
        same = (r >> 6) == (k >> 6)
        ang = (((r & 63) * (k & 63)) & 63).astype(F32) * (two_pi / FNET_GW)
        cs_ref[:, 0:FNET_CH] = jnp.where(same, jnp.cos(ang), 0.0).astype(BF16)
        cs_ref[:, FNET_CH:] = jnp.where(same, jnp.sin(ang), 0.0).astype(BF16)
        r = lax.broadcasted_iota(I32, (tr, tr), 0)
        k = lax.broadcasted_iota(I32, (tr, tr), 1)
        j_ref[...] = jnp.where(r + k == tr - 1, 1.0, 0.0).astype(BF16)

        def tab(rb, carry):
            r0 = pl.multiple_of(rb * SUBLANES, SUBLANES)
            j0 = lax.broadcasted_iota(I32, (SUBLANES, hk), 0) + r0
            kk = lax.broadcasted_iota(I32, (SUBLANES, hk), 1)
            a = ((j0 * kk) & (seq_len - 1)).astype(F32) * (two_pi / seq_len)
            cb_ref[pl.ds(r0, SUBLANES), :] = jnp.cos(a)
            sb_ref[pl.ds(r0, SUBLANES), :] = jnp.sin(a)
            return carry

        lax.fori_loop(0, tm // SUBLANES, tab, 0)

    @pl.when(i == 0)
    def _():
        nt = seq_len // tr
        w_ref[0:SUBLANES, :] = jnp.zeros((SUBLANES, 2 * FNET_CH), F32)
        for m in range(hk // tr):
            ub = u_ref[(nt - 1 - m) * tr:(nt - m) * tr, :].astype(BF16)
            urev = jnp.dot(j_ref[...], ub, preferred_element_type=F32).astype(BF16)
            w_ref[SUBLANES + m * tr:SUBLANES + (m + 1) * tr, :] = jnp.dot(
                urev, cs_ref[...], preferred_element_type=F32)
        first_row = lax.broadcasted_iota(I32, (tr, 1), 0) == 0
        for m in range(hk // tr):
            sl = slice(m * tr, (m + 1) * tr)
            v = jnp.dot(u_ref[sl, :].astype(BF16), cs_ref[...], preferred_element_type=F32)
            mirror = w_ref[pl.ds(SUBLANES + m * tr - 1, tr), :]
            if m == 0:
                mirror = jnp.where(first_row, 0.0, mirror)
            vc_ref[sl, :] = (v[:, 0:FNET_CH] + mirror[:, 0:FNET_CH]).astype(BF16)
            vs_ref[sl, :] = (v[:, FNET_CH:] - mirror[:, FNET_CH:]).astype(BF16)
        vmid_ref[...] = jnp.dot(u_ref[hk:hk + SUBLANES, :].astype(BF16), cs_ref[:, 0:FNET_CH],
                                preferred_element_type=F32)

    kk = lax.broadcasted_iota(I32, (1, hk), 1)
    ang = (((i * tm) * kk) & (seq_len - 1)).astype(F32) * (two_pi / seq_len)
    ca = jnp.cos(ang)
    sa = jnp.sin(ang)
    parity = (i * tm + lax.broadcasted_iota(I32, (tm, 1), 0)) & 1
    acc = (1 - 2 * parity).astype(F32) * vmid_ref[0:1, :]
    for kci in range(hk // kc):
        sl = slice(kci * kc, (kci + 1) * kc)
        cb = cb_ref[:, sl]
        sb = sb_ref[:, sl]
        cmat = (ca[:, sl] * cb - sa[:, sl] * sb).astype(BF16)
        smat = (sa[:, sl] * cb + ca[:, sl] * sb).astype(BF16)
        acc = acc + jnp.dot(cmat, vc_ref[sl, :], preferred_element_type=F32)
        acc = acc - jnp.dot(smat, vs_ref[sl, :], preferred_element_type=F32)
    zf = acc * (1.0 / math.sqrt(seq_len * FNET_GW))
    o_ref[...] = jnp.dot(zf.astype(BF16), wf_ref[...], preferred_element_type=F32) + bf_ref[...]


def _fnet(z, n_seq, seq_len, w_blk, b_vec):
    tm = 256
    hk = seq_len // 2
    kc = min(hk, 1024)
    tr = min(hk, 256)
    ni = seq_len // tm
    n = n_seq * seq_len
    col = (D - FNET_CH) // FNET_CH
    return pl.pallas_call(
        functools.partial(_fnet_kernel, seq_len=seq_len, tm=tm, kc=kc, tr=tr),
        out_shape=jax.ShapeDtypeStruct((n, FNET_CH), F32),
        grid=(n_seq, ni),
        in_specs=[pl.BlockSpec((seq_len, FNET_CH), lambda s, i: (s, col)),
                  pl.BlockSpec((FNET_CH, FNET_CH), lambda s, i: (0, 0)),
                  pl.BlockSpec((1, FNET_CH), lambda s, i: (0, 0))],
        out_specs=pl.BlockSpec((tm, FNET_CH), lambda s, i: (s * ni + i, 0)),
        scratch_shapes=[pltpu.VMEM((FNET_CH, 2 * FNET_CH), BF16),
                        pltpu.VMEM((tr, tr), BF16),
                        pltpu.VMEM((tm, hk), F32),
                        pltpu.VMEM((tm, hk), F32),
                        pltpu.VMEM((hk, FNET_CH), BF16),
                        pltpu.VMEM((hk, FNET_CH), BF16),
                        pltpu.VMEM((hk + SUBLANES, 2 * FNET_CH), F32),
                        pltpu.VMEM((SUBLANES, FNET_CH), F32)],
        compiler_params=_cparams(("arbitrary", "arbitrary")),
        name="fnet",
    )(z, w_blk, b_vec)


def _post0_kernel(*refs, with_pos, tm):
    if with_pos:
        (yf_ref, yb_ref, u_ref, fn_ref, x_ref, tab_ref, d_ref, wglu_ref, bglu_ref, woa_ref,
         wob_ref, gpost_ref, gfpre_ref, gfpost_ref, m_ref, w1_ref, w2_ref, o_ref, a_ref, xs_ref) = refs
        _add_pos(x_ref, tab_ref, xs_ref, tm)
        x_ref = xs_ref
    else:
        (yf_ref, yb_ref, u_ref, fn_ref, x_ref, d_ref, wglu_ref, bglu_ref, woa_ref,
         wob_ref, gpost_ref, gfpre_ref, gfpost_ref, m_ref, w1_ref, w2_ref, o_ref, a_ref) = refs
    m = m_ref[0]
    for rows in _row_parts(tm, POST0_SPLIT):
        y = yf_ref[rows, :] + yb_ref[rows, :] + d_ref[...] * u_ref[rows, :]
        g = jax.nn.gelu(y)
        gate = jnp.dot(g.astype(BF16), wglu_ref[...], preferred_element_type=F32) + bglu_ref[...]
        ya = g * jax.nn.sigmoid(gate)
        o = jnp.dot(ya.astype(BF16), woa_ref[...], preferred_element_type=F32)
        o = o + jnp.dot(fn_ref[rows, :].astype(BF16), wob_ref[...], preferred_element_type=F32)
        x = x_ref[rows, :] + _mod(m, 2) * _rms(o, gpost_ref[...])
        o_ref[rows, :] = _ffn_sublayer(x, m, gfpre_ref[...], gfpost_ref[...], w1_ref, w2_ref,
                                       a_ref, rows)


def _post0(yf, yb, z, yfn, x, tab, mods, cond_map, p, tm=TM):
    n = x.shape[0]
    with_pos = tab is not None
    vec = lambda a, w: a.reshape(1, w)
    args = [yf, yb, z, yfn, x] + ([tab] if with_pos else [])
    in_specs = [_rows(S5_CH, tm)] * 3 + [_rows(FNET_CH, tm), _rows(D, tm)]
    in_specs += [_resident(tab.shape)] if with_pos else []
    consts = [vec(p["l0_s5_d"], S5_CH), p["l0_s5_w_glu"], vec(p["l0_s5_b_glu"], S5_CH),
              p["l0_w_out_a"], p["l0_w_out_b"], vec(p["l0_g_mix_post"], D),
              vec(p["l0_g_ff_pre"], D), vec(p["l0_g_ff_post"], D)]
    args += consts + [mods, p["l0_w_ff1"], p["l0_w_ff2"]]
    in_specs += [_resident(a.shape) for a in consts] + [_mod_spec(cond_map, tm),
                                                        _resident((D, D_FF)), _resident((D_FF, D))]
    scratch = [pltpu.VMEM((tm, D_FF), BF16)] + ([pltpu.VMEM((tm, D), F32)] if with_pos else [])
    return pl.pallas_call(
        functools.partial(_post0_kernel, with_pos=with_pos, tm=tm),
        out_shape=jax.ShapeDtypeStruct((n, D), F32),
        grid=(n // tm,),
        in_specs=in_specs,
        out_specs=_rows(D, tm),
        scratch_shapes=scratch,
        compiler_params=_cparams(("parallel",)),
        name="post0",
    )(*args)


def _layer1_kernel(x_ref, xp_ref, xn_ref, gpre_ref, m_ref, win_ref, pw_ref, ps_ref, lng_ref,
                   lnb_ref, ws_ref, bs_ref, wout_ref, gpost_ref, gfpre_ref, gfpost_ref,
                   w1_ref, w2_ref, o_ref, xe_ref, y_ref, a_ref, *, tm, seq_len):
    i = pl.program_id(0)
    hl = POOL_HALO
    m = m_ref[0]
    x = x_ref[...]
    scale = 1.0 + _mod(m, 1)
    shift = _mod(m, 0)
    h = (_rms(x, gpre_ref[...]) * scale + shift).astype(BF16)

    r0 = i * tm
    prev_ok = (r0 & (seq_len - 1)) != 0
    next_ok = ((r0 + tm) & (seq_len - 1)) != 0
    xh = jnp.concatenate([xp_ref[...], xn_ref[...]], axis=0)
    hh = (_rms(xh, gpre_ref[...]) * scale + shift).astype(BF16)
    zh = jnp.dot(hh, win_ref[:, 0:POOL_CH], preferred_element_type=F32)
    xe_ref[0:hl, :] = jnp.where(prev_ok, zh[0:hl], 0.0)
    xe_ref[hl + tm:hl + tm + hl, :] = jnp.where(next_ok, zh[hl:], 0.0)
    xe_ref[hl:hl + tm, :] = jnp.dot(h, win_ref[:, 0:POOL_CH], preferred_element_type=F32)

    for rows in _row_parts(tm, LAYER1_SPLIT):
        part = rows.stop - rows.start
        t = (r0 + rows.start + lax.broadcasted_iota(I32, (part, 1), 0)) & (seq_len - 1)
        for gi, win in enumerate(POOL_WINDOWS):
            half = win // 2
            cols = slice(gi * POOL_GW, (gi + 1) * POOL_GW)
            acc = None
            for o in range(-half, half):
                v = xe_ref[pl.ds(hl + rows.start + o, part), cols]
                if o != 0:
                    v = jnp.where((t >= -o) if o < 0 else (t < seq_len - o), v, 0.0)
                acc = v if acc is None else acc + v
            cnt = (jnp.minimum(t + half, seq_len) - jnp.maximum(t - half, 0)).astype(F32)
            pooled = acc / cnt - xe_ref[pl.ds(hl + rows.start, part), cols]
            mixed = jnp.dot(pooled.astype(BF16), pw_ref[gi], preferred_element_type=F32)
            y_ref[rows, cols] = (mixed * ps_ref[:, cols]).astype(BF16)

        hp = h[rows, :]
        zu = jax.nn.gelu(jnp.dot(hp, win_ref[:, POOL_CH:POOL_CH + GMLP_CH],
                                 preferred_element_type=F32))
        zv = jax.nn.gelu(jnp.dot(hp, win_ref[:, POOL_CH + GMLP_CH:], preferred_element_type=F32))
        for hd in range(GMLP_HEADS):
            hc = slice(hd * GMLP_HD, (hd + 1) * GMLP_HD)
            v = zv[:, hc]
            vc = v - jnp.mean(v, axis=-1, keepdims=True)
            var = jnp.mean(vc * vc, axis=-1, keepdims=True)
            vn = (vc * lax.rsqrt(var + EPS) * lng_ref[:, hc] + lnb_ref[:, hc]).astype(BF16)
            for ch in range(part // GMLP_CHUNK):
                cr = slice(ch * GMLP_CHUNK, (ch + 1) * GMLP_CHUNK)
                sv = (jnp.dot(ws_ref[hd], vn[cr, :], preferred_element_type=F32)
                      + bs_ref[:, hd:hd + 1])
                y_ref[pl.ds(rows.start + ch * GMLP_CHUNK, GMLP_CHUNK),
                      POOL_CH + hd * GMLP_HD:POOL_CH + (hd + 1) * GMLP_HD] = (
                    zu[cr, hc] * sv).astype(BF16)

        o = jnp.dot(y_ref[rows, :], wout_ref[...], preferred_element_type=F32)
        xm = x[rows, :] + _mod(m, 2) * _rms(o, gpost_ref[...])
        o_ref[rows, :] = _ffn_sublayer(xm, m, gfpre_ref[...], gfpost_ref[...], w1_ref, w2_ref,
                                       a_ref, rows)


def _layer1(x, seq_len, mods, cond_map, p, tm=TM):
    n = x.shape[0]
    hb = tm // POOL_HALO
    nhb = n // POOL_HALO
    vec = lambda a, w: a.reshape(1, w)
    consts_a = [vec(p["l1_g_mix_pre"], D)]
    consts_b = [p["l1_w_in"], p["pool_w"], vec(p["l1_pool_scale"], POOL_CH),
                vec(p["l1_gmlp_ln_g"], GMLP_CH), vec(p["l1_gmlp_ln_b"], GMLP_CH), p["gmlp_ws"],
                p["gmlp_bs_t"], p["l1_w_out"], vec(p["l1_g_mix_post"], D),
                vec(p["l1_g_ff_pre"], D), vec(p["l1_g_ff_post"], D), p["l1_w_ff1"], p["l1_w_ff2"]]
    in_specs = [_rows(D, tm),
                pl.BlockSpec((POOL_HALO, D), lambda i: (jnp.maximum(i * hb - 1, 0), 0)),
                pl.BlockSpec((POOL_HALO, D), lambda i: (jnp.minimum((i + 1) * hb, nhb - 1), 0))]
    in_specs += [_resident(a.shape) for a in consts_a] + [_mod_spec(cond_map, tm)]
    in_specs += [_resident(a.shape) for a in consts_b]
    return pl.pallas_call(
        functools.partial(_layer1_kernel, tm=tm, seq_len=seq_len),
        out_shape=jax.ShapeDtypeStruct((n, D), F32),
        grid=(n // tm,),
        in_specs=in_specs,
        out_specs=_rows(D, tm),
        scratch_shapes=[pltpu.VMEM((tm + 2 * POOL_HALO, POOL_CH), F32),
                        pltpu.VMEM((tm, D), BF16),
                        pltpu.VMEM((tm, D_FF), BF16)],
        compiler_params=_cparams(("parallel",)),
        name="layer1",
    )(x, x, x, *consts_a, mods, *consts_b)


def _trunk(x, tab, n_seq, seq_len, cond_map, mods0, mods1, h0, p):
    z = _inproj(x, tab, p["l0_g_mix_pre"], mods0, cond_map, p["l0_w_in"])
    yf, yb, hfin = _s5_scan(z, n_seq, seq_len, h0, p["a_vec"], p["b_blk"], p["c_blk"])
    yfn = _fnet(z, n_seq, seq_len, p["fnet_w_blk"], p["fnet_b"])
    x = _post0(yf, yb, z, yfn, x, tab, mods0, cond_map, p)
    x = _layer1(x, seq_len, mods1, cond_map, p)
    return x, hfin


def kernel(x_prompt, x_sample, state_l0_s5_re, state_l0_s5_im, c, c_ctx, l0_w_mod, l0_b_mod, l0_g_mix_pre, l0_g_mix_post, l0_g_ff_pre, l0_g_ff_post, l0_w_ff1, l0_w_ff2, l0_w_in, l0_w_out, l0_s5_lambda_re, l0_s5_lambda_im, l0_s5_log_dt, l0_s5_b_re, l0_s5_b_im, l0_s5_c_re, l0_s5_c_im, l0_s5_d, l0_s5_w_glu, l0_s5_b_glu, l0_fnet_w, l0_fnet_b, l1_w_mod, l1_b_mod, l1_g_mix_pre, l1_g_mix_post, l1_g_ff_pre, l1_g_ff_post, l1_w_ff1, l1_w_ff2, l1_w_in, l1_w_out, l1_pool_w, l1_pool_scale, l1_gmlp_ln_g, l1_gmlp_ln_b, l1_gmlp_ws, l1_gmlp_bs):
    batch, seq, _ = x_prompt.shape
    dec_batch, dec_seq, _ = x_sample.shape
    assert dec_seq == GRID_W * GRID_W and seq % S5_TT == 0 and TM % seq == 0

    conds = jnp.zeros((8, D), F32).at[0].set(c_ctx).at[1:1 + dec_batch].set(c)
    mods0 = _modulation(conds, l0_w_mod, l0_b_mod)
    mods1 = _modulation(conds, l1_w_mod, l1_b_mod)

    b_blk, c_blk, a_vec = _s5_params(l0_s5_lambda_re, l0_s5_lambda_im, l0_s5_log_dt,
                                     l0_s5_b_re, l0_s5_b_im, l0_s5_c_re, l0_s5_c_im)

    eye4 = jnp.eye(FNET_CH // FNET_GW, dtype=F32)
    fnet_w_blk = jnp.einsum("gcd,gk->gckd", l0_fnet_w, eye4).reshape(FNET_CH, FNET_CH).astype(BF16)

    p = dict(
        l0_g_mix_pre=l0_g_mix_pre, l0_g_mix_post=l0_g_mix_post, l0_g_ff_pre=l0_g_ff_pre,
        l0_g_ff_post=l0_g_ff_post, l0_w_ff1=l0_w_ff1.astype(BF16), l0_w_ff2=l0_w_ff2.astype(BF16),
        l0_w_in=l0_w_in.astype(BF16), l0_w_out_a=l0_w_out[:S5_CH].astype(BF16),
        l0_w_out_b=l0_w_out[S5_CH:].astype(BF16), l0_s5_d=l0_s5_d,
        l0_s5_w_glu=l0_s5_w_glu.astype(BF16), l0_s5_b_glu=l0_s5_b_glu,
        a_vec=a_vec, b_blk=b_blk, c_blk=c_blk, fnet_w_blk=fnet_w_blk,
        fnet_b=l0_fnet_b.reshape(1, FNET_CH),
        l1_g_mix_pre=l1_g_mix_pre, l1_g_mix_post=l1_g_mix_post, l1_g_ff_pre=l1_g_ff_pre,
        l1_g_ff_post=l1_g_ff_post, l1_w_ff1=l1_w_ff1.astype(BF16), l1_w_ff2=l1_w_ff2.astype(BF16),
        l1_w_in=l1_w_in.astype(BF16), l1_w_out=l1_w_out.astype(BF16),
        pool_w=l1_pool_w.astype(BF16), l1_pool_scale=l1_pool_scale, l1_gmlp_ln_g=l1_gmlp_ln_g,
        l1_gmlp_ln_b=l1_gmlp_ln_b, gmlp_ws=l1_gmlp_ws.astype(BF16), gmlp_bs_t=l1_gmlp_bs.T,
    )

    zero_state = jnp.zeros((batch, 12, SUBLANES, LANES), F32)
    y_prompt, hfin = _trunk(x_prompt.reshape(batch * seq, D), None, batch, seq, lambda r: 0,
                            mods0, mods1, zero_state, p)
    new_re, new_im = _unpack_state(hfin)

    h0 = _pack_state(state_l0_s5_re, state_l0_s5_im)
    y_sample, _ = _trunk(x_sample.reshape(dec_batch * dec_seq, D), _pos_table(), dec_batch, dec_seq,
                         lambda r: 1 + r // dec_seq, mods0, mods1, h0, p)

    return (y_prompt.reshape(batch, seq, D), y_sample.reshape(dec_batch, dec_seq, D), new_re, new_im)
```
